```python
import math
import jax, jax.numpy as jnp
from jax import lax
import numpy as np

D_MODEL = 2048
BATCH = 16
SEQ = 256
DEPTH = 2
DEC_BATCH = 2
DEC_SEQ = 4096
PAST_LEN = 512

GRID_W = 64
ROPE_BASE = 10000.0
NORM_EPS = 1e-6
Q_BLOCK = 128
N_BRANCHES = 3
BRANCH_WIDTH = 1024

DIFF_HEADS = 8
DIFF_QK_DIM = 64
DIFF_V_DIM = 128

HG_HEADS = 8
HG_KEY_DIM = 128
HG_VAL_DIM = 128
HG_CHUNK = 64
F_FLOOR = 1e-30

WIN_Q_HEADS = 8
WIN_KV_HEADS = 4
WIN_GROUP = WIN_Q_HEADS // WIN_KV_HEADS
WIN_HEAD_DIM = 128
WINDOW = 128

DIFF_QK_W = DIFF_HEADS * 2 * DIFF_QK_DIM
DIFF_V_W = DIFF_HEADS * DIFF_V_DIM
HG_K_W = HG_HEADS * HG_KEY_DIM
HG_V_W = HG_HEADS * HG_VAL_DIM
WIN_Q_W = WIN_Q_HEADS * WIN_HEAD_DIM
WIN_KV_W = WIN_KV_HEADS * WIN_HEAD_DIM
IN_SPLITS = (DIFF_QK_W, DIFF_QK_W, DIFF_V_W, DIFF_V_W,
             HG_K_W, HG_K_W, HG_K_W, HG_V_W, HG_V_W,
             WIN_Q_W, WIN_KV_W, WIN_KV_W, WIN_Q_W,
             N_BRANCHES * D_MODEL)
N_IN = sum(IN_SPLITS)

kernel_name = 'hybrid_diffusion_trunk_step'


def rms_norm(x, gain):
    xf = x.astype(jnp.float32)
    y = xf * lax.rsqrt(jnp.mean(xf * xf, axis=-1, keepdims=True) + NORM_EPS)
    return (y * gain.astype(jnp.float32)).astype(x.dtype)


def axial_rope_tables(T, d):
    rows = T // GRID_W
    row = jnp.repeat(jnp.arange(rows, dtype=jnp.float32), GRID_W)
    col = jnp.tile(jnp.arange(GRID_W, dtype=jnp.float32), rows)
    quarter = d // 4
    inv = ROPE_BASE ** (-jnp.arange(quarter, dtype=jnp.float32) / quarter)
    ar = row[:, None] * inv
    ac = col[:, None] * inv
    ang = jnp.concatenate([ar, ar, ac, ac], axis=-1)
    return jnp.cos(ang), jnp.sin(ang)


def apply_rope(x, cos, sin):
    d = x.shape[-1]
    shape = (1, x.shape[1]) + (1,) * (x.ndim - 3) + (d,)
    xr = x.reshape(x.shape[:-1] + (2, 2, d // 4))
    rot = jnp.stack([-xr[..., 1, :], xr[..., 0, :]], axis=-2).reshape(x.shape)
    return x * cos.reshape(shape).astype(x.dtype) + rot * sin.reshape(shape).astype(x.dtype)


def diff_lambda(lam_vecs, l):
    lv = lam_vecs.astype(jnp.float32)
    lam_init = 0.8 - 0.6 * math.exp(-0.3 * l)
    lam = jnp.exp(jnp.sum(lv[0] * lv[1])) - jnp.exp(jnp.sum(lv[2] * lv[3])) + lam_init
    return lam, lam_init


def diff_attend(q, k, v, lam):
    B, T = q.shape[:2]
    nb = T // Q_BLOCK
    qb = jnp.moveaxis(q.reshape((B, nb, Q_BLOCK) + q.shape[2:]), 1, 0)
    scale = q.shape[-1] ** -0.5

    def block(qblk):
        s = jnp.einsum('bqhcd,bkhcd->bhcqk', qblk, k).astype(jnp.float32) * scale
        p = jax.nn.softmax(s, axis=-1)
        pd = p[:, :, 0] - lam * p[:, :, 1]
        return jnp.einsum('bhqk,bkhe->bqhe', pd.astype(v.dtype), v)

    out = lax.map(block, qb)
    return jnp.moveaxis(out, 0, 1).reshape(B, T, q.shape[2], v.shape[-1])


def sink_attend_dense(q, k, v, sink):
    B, T = q.shape[:2]
    nb = T // Q_BLOCK
    qb = jnp.moveaxis(q.reshape((B, nb, Q_BLOCK) + q.shape[2:]), 1, 0)
    scale = q.shape[-1] ** -0.5
    s_sink = jnp.broadcast_to(sink.astype(jnp.float32)[None, :, :, None, None],
                              (B, WIN_KV_HEADS, WIN_GROUP, Q_BLOCK, 1))

    def block(qblk):
        s = jnp.einsum('bqhgd,bkhd->bhgqk', qblk, k).astype(jnp.float32) * scale
        p = jax.nn.softmax(jnp.concatenate([s, s_sink], axis=-1), axis=-1)[..., :-1]
        return jnp.einsum('bhgqk,bkhd->bqhgd', p.astype(v.dtype), v)

    out = lax.map(block, qb)
    return jnp.moveaxis(out, 0, 1).reshape(q.shape)


def window_attend(q, k, v, k_ctx, v_ctx, sink):
    B, T = q.shape[:2]
    nb = T // WINDOW
    qb = q.reshape(B, nb, WINDOW, WIN_KV_HEADS, WIN_GROUP, WIN_HEAD_DIM)

    def band(t):
        tp = jnp.pad(t, ((0, 0), (WINDOW, WINDOW), (0, 0), (0, 0)))
        tp = tp.reshape(B, nb + 2, WINDOW, WIN_KV_HEADS, WIN_HEAD_DIM)
        return jnp.concatenate([tp[:, :-2], tp[:, 1:-1], tp[:, 2:]], axis=2)

    kb, vb = band(k), band(v)
    blk = jnp.arange(nb)[:, None]
    q_pos = blk * WINDOW + jnp.arange(WINDOW)[None]
    kp = ((blk - 1) * WINDOW + jnp.arange(3 * WINDOW)[None])[:, None, :]
    mask = (kp >= 0) & (kp < T) & (jnp.abs(kp - q_pos[:, :, None]) <= WINDOW)
    scale = WIN_HEAD_DIM ** -0.5
    s_loc = jnp.einsum('bnqhgd,bnkhd->bnhgqk', qb, kb).astype(jnp.float32) * scale
    s_loc = jnp.where(mask[None, :, None, None], s_loc, -jnp.inf)
    s_ctx = jnp.einsum('bnqhgd,bkhd->bnhgqk', qb, k_ctx).astype(jnp.float32) * scale
    s_sink = jnp.broadcast_to(sink.astype(jnp.float32)[None, None, :, :, None, None], s_loc.shape[:-1] + (1,))
    p = jax.nn.softmax(jnp.concatenate([s_loc, s_ctx, s_sink], axis=-1), axis=-1).astype(v.dtype)
    n_loc = 3 * WINDOW
    out = (jnp.einsum('bnhgqk,bnkhd->bnqhgd', p[..., :n_loc], vb)
           + jnp.einsum('bnhgqk,bkhd->bnqhgd', p[..., n_loc:n_loc + k_ctx.shape[1]], v_ctx))
    return out.reshape(q.shape)


def hgrn_lower_bounds(lb_param):
    p = jax.nn.softmax(lb_param.astype(jnp.float32), axis=0)
    return jnp.cumsum(p, axis=0) - p[0]


def hgrn_forget(z, lb):
    zf = z.astype(jnp.float32)
    f = lb + (1.0 - lb) * jax.nn.sigmoid(zf)
    logf = jnp.log(jnp.maximum(f, F_FLOOR))
    k = (1.0 - lb) * jax.nn.sigmoid(-zf)
    return k, logf


def hgrn_chunk_scan(q, k, v, logf, s0):
    B, T, H, dk = q.shape
    dv = v.shape[-1]
    n = T // HG_CHUNK

    def chunks(t):
        return t.astype(jnp.float32).reshape(B, n, HG_CHUNK, H, t.shape[-1]).transpose(1, 0, 3, 2, 4)

    causal = jnp.tril(jnp.ones((HG_CHUNK, HG_CHUNK), dtype=bool))

    def step(S, inp):
        qc, kc, vc, lf = inp
        b = jnp.cumsum(lf, axis=2)
        b_last = b[:, :, -1:, :]
        inter = jnp.einsum('bhtd,bhde->bhte', qc * jnp.exp(b), S)
        rel = jnp.where(causal[:, :, None], b[:, :, :, None, :] - b[:, :, None, :, :], -jnp.inf)
        att = jnp.einsum('bhtd,bhsd,bhtsd->bhts', qc, kc, jnp.exp(rel))
        o = inter + jnp.einsum('bhts,bhse->bhte', att, vc)
        S_new = (jnp.exp(b_last[:, :, 0, :])[..., None] * S
                 + jnp.einsum('bhsd,bhse->bhde', kc * jnp.exp(b_last - b), vc))
        return S_new, o

    s_fin, o = lax.scan(step, s0.astype(jnp.float32), (chunks(q), chunks(k), chunks(v), chunks(logf)))
    o = o.transpose(1, 0, 3, 2, 4).reshape(B, T, H, dv)
    return o.astype(v.dtype), s_fin


def hgrn_bidir(q, v, k_f, lf_f, k_b, lf_b, s0_f, s0_b):
    o_f, s_f = hgrn_chunk_scan(q, k_f, v, lf_f, s0_f)
    o_b, s_b = hgrn_chunk_scan(q[:, ::-1], k_b[:, ::-1], v[:, ::-1], lf_b[:, ::-1], s0_b)
    return o_f + o_b[:, ::-1], s_f, s_b


def prep_branches(x, cond, l, w, lb_f, lb_b, ropes):
    B, T, _ = x.shape
    mod = jax.nn.silu(cond) @ w['w_ada'][l] + w['b_ada'][l]
    shift, scale, gate = jnp.split(mod[:, None, :], 3, axis=-1)
    h = rms_norm(x, w['norm_gain'][l]) * (1 + scale) + shift
    offsets = np.cumsum(IN_SPLITS)[:-1].tolist()
    (dq, dk, dv, dg, hq, hff, hfb, hi, hg, wq, wk, wv, wg, mg) = jnp.split(h @ w['w_in'][l], offsets, axis=-1)
    dq = rms_norm(dq.reshape(B, T, DIFF_HEADS, 2, DIFF_QK_DIM), w['diff_q_norm'][l])
    dk = rms_norm(dk.reshape(B, T, DIFF_HEADS, 2, DIFF_QK_DIM), w['diff_k_norm'][l])
    dv = dv.reshape(B, T, DIFF_HEADS, DIFF_V_DIM)
    wq = rms_norm(wq.reshape(B, T, WIN_KV_HEADS, WIN_GROUP, WIN_HEAD_DIM), w['win_q_norm'][l])
    wk = rms_norm(wk.reshape(B, T, WIN_KV_HEADS, WIN_HEAD_DIM), w['win_k_norm'][l])
    wv = wv.reshape(B, T, WIN_KV_HEADS, WIN_HEAD_DIM)
    if ropes is not None:
        cos_d, sin_d, cos_w, sin_w = ropes
        dq, dk = apply_rope(dq, cos_d, sin_d), apply_rope(dk, cos_d, sin_d)
        wq, wk = apply_rope(wq, cos_w, sin_w), apply_rope(wk, cos_w, sin_w)
    hq = hq.reshape(B, T, HG_HEADS, HG_KEY_DIM)
    hi = hi.reshape(B, T, HG_HEADS, HG_VAL_DIM)
    k_f, lf_f = hgrn_forget(hff.reshape(B, T, HG_HEADS, HG_KEY_DIM), lb_f)
    k_b, lf_b = hgrn_forget(hfb.reshape(B, T, HG_HEADS, HG_KEY_DIM), lb_b)
    return gate, (dq, dk, dv), (hq, hi, k_f, lf_f, k_b, lf_b), (wq, wk, wv), (dg, hg, wg), mg


def finish_layer(x, gate, yd, yh, yw, gates, mg, l, w):
    B, T, _ = x.shape
    ys = jnp.stack([yd, yh, yw], axis=2) * jax.nn.silu(jnp.stack(gates, axis=2))
    br = jnp.einsum('btnw,nwd->btnd', ys, w['w_branch'][l])
    merged = jnp.sum(jax.nn.sigmoid(mg.reshape(B, T, N_BRANCHES, D_MODEL)) * br, axis=2)
    return x + gate * (merged @ w['w_out'][l])


def context_layer(x, c_ctx, l, w, lb_f, lb_b):
    B, T, _ = x.shape
    gate, (dq, dk, dv), (hq, hi, k_f, lf_f, k_b, lf_b), (wq, wk, wv), gates, mg = prep_branches(
        x, c_ctx[None], l, w, lb_f, lb_b, None)
    lam, lam_init = diff_lambda(w['diff_lambda'][l], l)
    yd = (rms_norm(diff_attend(dq, dk, dv, lam), w['diff_subln'][l]) * (1 - lam_init)).reshape(B, T, DIFF_V_W)
    s0 = jnp.zeros((B, HG_HEADS, HG_KEY_DIM, HG_VAL_DIM), jnp.float32)
    oh, s_f, s_b = hgrn_bidir(hq, hi, k_f, lf_f, k_b, lf_b, s0, s0)
    yh = rms_norm(oh, w['hg_out_norm'][l]).reshape(B, T, HG_V_W)
    yw = sink_attend_dense(wq, wk, wv, w['win_sink'][l].reshape(WIN_KV_HEADS, WIN_GROUP)).reshape(B, T, WIN_Q_W)
    y = finish_layer(x, gate, yd, yh, yw, gates, mg, l, w)
    return y, (dk, dv, wk, wv, jnp.stack([s_f, s_b], axis=1))


def latent_layer(x, c, l, w, lb_f, lb_b, ropes, ck_d, cv_d, ck_w, cv_w, st):
    B, T, _ = x.shape
    gate, (dq, dk, dv), (hq, hi, k_f, lf_f, k_b, lf_b), (wq, wk, wv), gates, mg = prep_branches(
        x, c, l, w, lb_f, lb_b, ropes)
    lam, lam_init = diff_lambda(w['diff_lambda'][l], l)
    od = diff_attend(dq, jnp.concatenate([dk, ck_d], axis=1), jnp.concatenate([dv, cv_d], axis=1), lam)
    yd = (rms_norm(od, w['diff_subln'][l]) * (1 - lam_init)).reshape(B, T, DIFF_V_W)
    oh, _, _ = hgrn_bidir(hq, hi, k_f, lf_f, k_b, lf_b, st[:, 0], st[:, 1])
    yh = rms_norm(oh, w['hg_out_norm'][l]).reshape(B, T, HG_V_W)
    yw = window_attend(wq, wk, wv, ck_w, cv_w, w['win_sink'][l].reshape(WIN_KV_HEADS, WIN_GROUP)).reshape(B, T, WIN_Q_W)
    return finish_layer(x, gate, yd, yh, yw, gates, mg, l, w)


def setup_inputs(seed: int = 0) -> dict:
    key = jax.random.key(seed)
    ks = jax.random.split(key, 24)
    f32 = jnp.float32

    def nrm(k, shape, s=1.0):
        return jax.random.normal(k, shape, f32) * s

    return {
        'x_prompt': nrm(ks[0], (BATCH, SEQ, D_MODEL)),
        'x_sample': nrm(ks[1], (DEC_BATCH, DEC_SEQ, D_MODEL)),
        'cache_diff_k': nrm(ks[2], (DEC_BATCH, DEPTH, PAST_LEN, DIFF_HEADS, 2, DIFF_QK_DIM)),
        'cache_diff_v': nrm(ks[3], (DEC_BATCH, DEPTH, PAST_LEN, DIFF_HEADS, DIFF_V_DIM)),
        'cache_win_k': nrm(ks[4], (DEC_BATCH, DEPTH, PAST_LEN, WIN_KV_HEADS, WIN_HEAD_DIM)),
        'cache_win_v': nrm(ks[5], (DEC_BATCH, DEPTH, PAST_LEN, WIN_KV_HEADS, WIN_HEAD_DIM)),
        'state_hgrn': nrm(ks[6], (DEC_BATCH, DEPTH, 2, HG_HEADS, HG_KEY_DIM, HG_VAL_DIM), 0.5),
        'c': nrm(ks[7], (DEC_BATCH, D_MODEL)),
        'c_ctx': nrm(ks[8], (D_MODEL,)),
        'norm_gain': 1.0 + nrm(ks[9], (DEPTH, D_MODEL), 0.02),
        'w_ada': nrm(ks[10], (DEPTH, D_MODEL, 3 * D_MODEL), 0.5 * D_MODEL ** -0.5),
        'b_ada': nrm(ks[11], (DEPTH, 3 * D_MODEL), 0.01),
        'w_in': nrm(ks[12], (DEPTH, D_MODEL, N_IN), D_MODEL ** -0.5),
        'diff_q_norm': 1.0 + nrm(ks[13], (DEPTH, DIFF_QK_DIM), 0.02),
        'diff_k_norm': 1.0 + nrm(ks[14], (DEPTH, DIFF_QK_DIM), 0.02),
        'diff_lambda': nrm(ks[15], (DEPTH, 4, DIFF_QK_DIM), 0.1),
        'diff_subln': 1.0 + nrm(ks[16], (DEPTH, DIFF_V_DIM), 0.02),
        'hg_lb': nrm(ks[17], (2, DEPTH, HG_K_W), 0.5),
        'hg_out_norm': 1.0 + nrm(ks[18], (DEPTH, HG_VAL_DIM), 0.02),
        'win_q_norm': 1.0 + nrm(ks[19], (DEPTH, WIN_HEAD_DIM), 0.02),
        'win_k_norm': 1.0 + nrm(ks[20], (DEPTH, WIN_HEAD_DIM), 0.02),
        'win_sink': nrm(ks[21], (DEPTH, WIN_Q_HEADS)),
        'w_branch': nrm(ks[22], (DEPTH, N_BRANCHES, BRANCH_WIDTH, D_MODEL), BRANCH_WIDTH ** -0.5),
        'w_out': nrm(ks[23], (DEPTH, D_MODEL, D_MODEL), D_MODEL ** -0.5),
    }


def reference(x_prompt, x_sample, cache_diff_k, cache_diff_v, cache_win_k, cache_win_v, state_hgrn,
              c, c_ctx, norm_gain, w_ada, b_ada, w_in, diff_q_norm, diff_k_norm, diff_lambda,
              diff_subln, hg_lb, hg_out_norm, win_q_norm, win_k_norm, win_sink, w_branch, w_out):
    w = {'norm_gain': norm_gain, 'w_ada': w_ada, 'b_ada': b_ada, 'w_in': w_in,
         'diff_q_norm': diff_q_norm, 'diff_k_norm': diff_k_norm, 'diff_lambda': diff_lambda,
         'diff_subln': diff_subln, 'hg_out_norm': hg_out_norm, 'win_q_norm': win_q_norm,
         'win_k_norm': win_k_norm, 'win_sink': win_sink, 'w_branch': w_branch, 'w_out': w_out}
    lb_fwd = hgrn_lower_bounds(hg_lb[0])
    lb_bwd = hgrn_lower_bounds(hg_lb[1])

    y_prompt = x_prompt
    dks, dvs, wks, wvs, sts = [], [], [], [], []
    for l in range(DEPTH):
        y_prompt, (dk_l, dv_l, wk_l, wv_l, st_l) = context_layer(
            y_prompt, c_ctx, l, w,
            lb_fwd[l].reshape(HG_HEADS, HG_KEY_DIM), lb_bwd[l].reshape(HG_HEADS, HG_KEY_DIM))
        dks.append(dk_l); dvs.append(dv_l); wks.append(wk_l); wvs.append(wv_l); sts.append(st_l)

    T = x_sample.shape[1]
    cos_d, sin_d = axial_rope_tables(T, DIFF_QK_DIM)
    cos_w, sin_w = axial_rope_tables(T, WIN_HEAD_DIM)
    ropes = (cos_d, sin_d, cos_w, sin_w)
    y_sample = x_sample
    for l in range(DEPTH):
        y_sample = latent_layer(
            y_sample, c, l, w,
            lb_fwd[l].reshape(HG_HEADS, HG_KEY_DIM), lb_bwd[l].reshape(HG_HEADS, HG_KEY_DIM), ropes,
            cache_diff_k[:, l], cache_diff_v[:, l], cache_win_k[:, l], cache_win_v[:, l], state_hgrn[:, l])

    new_diff_k = jnp.stack(dks, axis=1)
    new_diff_v = jnp.stack(dvs, axis=1)
    new_win_k = jnp.stack(wks, axis=1)
    new_win_v = jnp.stack(wvs, axis=1)
    new_hgrn_state = jnp.stack(sts, axis=1)
    return (y_prompt, y_sample, new_diff_k, new_diff_v, new_win_k, new_win_v, new_hgrn_state)
```

```python
import functools
import math

import numpy as np
import jax
import jax.numpy as jnp
from jax import lax
from jax.experimental import pallas as pl
from jax.experimental.pallas import tpu as pltpu

F32 = jnp.float32
BF16 = jnp.bfloat16

NORM_EPS = 1e-6
F_FLOOR = 1e-30
ROPE_BASE = 10000.0
GRID_W = 64
WINDOW = 128

LANES = 128
N_HEADS = 8
WIN_KV_HEADS = 4
WIN_GROUP = 2
DIFF_QK_DIM = 64
BRANCH_W = N_HEADS * LANES
HG_CHUNK = 64
HG_LEVELS = 6

OFF_DQ, OFF_DK, OFF_DV, OFF_DG = 0, 1024, 2048, 3072
OFF_HQ, OFF_HFF, OFF_HFB, OFF_HI, OFF_HG = 4096, 5120, 6144, 7168, 8192
OFF_WQ, OFF_WK, OFF_WV, OFF_WG = 9216, 10240, 10752, 11264
OFF_MG = 12288

VMEM_LIMIT = 48 * 1024 * 1024

NT_DIMS = (((1,), (1,)), ((), ()))
TN_DIMS = (((0,), (0,)), ((), ()))


def _params(*sem):
    return pltpu.CompilerParams(dimension_semantics=sem, vmem_limit_bytes=VMEM_LIMIT)


def _tile(n, preferred):
    t = min(n, preferred)
    while n % t:
        t //= 2
    return t


def _dot(a, b):
    return jnp.dot(a, b, preferred_element_type=F32)


def _dot_nt(a, b):
    return lax.dot_general(a, b, NT_DIMS, preferred_element_type=F32)


def _silu(x):
    return x * jax.nn.sigmoid(x)


def _split_bf16(x, terms):
    out = []
    for _ in range(terms - 1):
        hi = x.astype(BF16)
        out.append(hi)
        x = x - hi.astype(F32)
    out.append(x.astype(BF16))
    return out


def _ada_kernel(c_ref, w_ref, b_ref, o_ref):
    a = _silu(c_ref[...])
    a_hi, a_lo = _split_bf16(a, 2)
    w_hi, w_lo = _split_bf16(w_ref[...], 2)
    acc = _dot(a_hi, w_hi) + _dot(a_lo, w_hi) + _dot(a_hi, w_lo)
    o_ref[...] = acc + b_ref[...]


def _ada(cond, w_ada, b_ada):
    depth, d, n = w_ada.shape
    tn = _tile(n, 512)
    rows = cond.shape[0]
    return pl.pallas_call(
        _ada_kernel,
        grid=(depth, n // tn),
        in_specs=[
            pl.BlockSpec((rows, d), lambda l, j: (0, 0)),
            pl.BlockSpec((None, d, tn), lambda l, j: (l, 0, j)),
            pl.BlockSpec((None, 1, tn), lambda l, j: (l, 0, j)),
        ],
        out_specs=pl.BlockSpec((None, rows, tn), lambda l, j: (l, 0, j)),
        out_shape=jax.ShapeDtypeStruct((depth, rows, n), F32),
        compiler_params=_params("parallel", "parallel"),
        name="ada_mod",
    )(cond, w_ada, b_ada.reshape(depth, 1, n))


def _in_proj_kernel(x_ref, shift_ref, scale_ref, gain_ref, w_ref, o_ref, h_ref):
    @pl.when(pl.program_id(1) == 0)
    def _():
        x = x_ref[...]
        ms = jnp.mean(x * x, axis=-1, keepdims=True)
        y = x * lax.rsqrt(ms + NORM_EPS) * gain_ref[...]
        h_ref[...] = (y * (1.0 + scale_ref[...]) + shift_ref[...]).astype(BF16)

    o_ref[...] = _dot(h_ref[...], w_ref[...])


def _in_proj(x2d, shift, scale, gain, w_in, layer, rows_per_cond):
    m, d = x2d.shape
    n = w_in.shape[-1]
    tm = _tile(rows_per_cond, 1024)
    tn = _tile(n, 512)
    return pl.pallas_call(
        _in_proj_kernel,
        grid=(m // tm, n // tn),
        in_specs=[
            pl.BlockSpec((tm, d), lambda i, j: (i, 0)),
            pl.BlockSpec((None, 1, d), lambda i, j: (i * tm // rows_per_cond, 0, 0)),
            pl.BlockSpec((None, 1, d), lambda i, j: (i * tm // rows_per_cond, 0, 0)),
            pl.BlockSpec((1, d), lambda i, j: (0, 0)),
            pl.BlockSpec((None, d, tn), lambda i, j: (layer, 0, j)),
        ],
        out_specs=pl.BlockSpec((tm, tn), lambda i, j: (i, j)),
        out_shape=jax.ShapeDtypeStruct((m, n), F32),
        scratch_shapes=[pltpu.VMEM((tm, d), BF16)],
        compiler_params=_params("parallel", "arbitrary"),
        name="in_proj",
    )(x2d, shift, scale, gain, w_in)


def _head_norm(x, gain, group):
    x2 = x * x
    if group == LANES:
        ms = jnp.sum(x2, axis=-1, keepdims=True) * (1.0 / LANES)
    else:
        lo = lax.broadcasted_iota(jnp.int32, x.shape, 1) < group
        s_lo = jnp.sum(jnp.where(lo, x2, 0.0), axis=-1, keepdims=True)
        s_hi = jnp.sum(jnp.where(lo, 0.0, x2), axis=-1, keepdims=True)
        ms = jnp.where(lo, s_lo, s_hi) * (1.0 / group)
    return x * lax.rsqrt(ms + NORM_EPS) * gain


def _rope(y, cos, sin_up, sin_dn, quarter):
    return (y * cos + pltpu.roll(y, LANES - quarter, 1) * sin_up
            + pltpu.roll(y, quarter, 1) * sin_dn)


def _prep_kernel(*refs, rope, keep_f32):
    it = iter(refs)
    dq_ref, dk_ref, dv_ref, wq_ref, wk_ref, wv_ref = (next(it) for _ in range(6))
    gdq_ref, gdk_ref, gwq_ref, gwk_ref = (next(it) for _ in range(4))
    if rope:
        cd_ref, ud_ref, nd_ref, cw_ref, uw_ref, nw_ref = (next(it) for _ in range(6))
    odq, odk, odv, owq, owk, owv = (next(it) for _ in range(6))
    if keep_f32:
        odk32, owk32 = (next(it) for _ in range(2))

    def slab(src, j, gain_ref, group, quarter, tabs):
        y = _head_norm(src[:, j * LANES:(j + 1) * LANES], gain_ref[...], group)
        if rope:
            y = _rope(y, tabs[0][...], tabs[1][...], tabs[2][...], quarter)
        return y

    dtabs = (cd_ref, ud_ref, nd_ref) if rope else None
    wtabs = (cw_ref, uw_ref, nw_ref) if rope else None
    for j in range(N_HEADS):
        sl = slice(j * LANES, (j + 1) * LANES)
        q = slab(dq_ref, j, gdq_ref, DIFF_QK_DIM, DIFF_QK_DIM // 4, dtabs)
        odq[:, sl] = (q * (DIFF_QK_DIM ** -0.5)).astype(BF16)
        k = slab(dk_ref, j, gdk_ref, DIFF_QK_DIM, DIFF_QK_DIM // 4, dtabs)
        odk[:, sl] = k.astype(BF16)
        if keep_f32:
            odk32[:, sl] = k
        q = slab(wq_ref, j, gwq_ref, LANES, LANES // 4, wtabs)
        owq[:, sl] = (q * (LANES ** -0.5)).astype(BF16)
    for j in range(WIN_KV_HEADS):
        sl = slice(j * LANES, (j + 1) * LANES)
        k = slab(wk_ref, j, gwk_ref, LANES, LANES // 4, wtabs)
        owk[:, sl] = k.astype(BF16)
        if keep_f32:
            owk32[:, sl] = k
    odv[...] = dv_ref[...].astype(BF16)
    owv[...] = wv_ref[...].astype(BF16)


def _prep(proj, gains, rope_tabs, seq_len, keep_f32):
    m = proj.shape[0]
    tr = _tile(seq_len, 256)
    w8, w4 = BRANCH_W, WIN_KV_HEADS * LANES
    rope = rope_tabs is not None

    def col(width, off):
        return pl.BlockSpec((tr, width), lambda i: (i, off // width))

    small = pl.BlockSpec((1, LANES), lambda i: (0, 0))
    in_specs = [col(w8, OFF_DQ), col(w8, OFF_DK), col(w8, OFF_DV), col(w8, OFF_WQ),
                col(w4, OFF_WK), col(w4, OFF_WV), small, small, small, small]
    args = [proj] * 6 + list(gains)
    if rope:
        nt = seq_len // tr
        tab = pl.BlockSpec((tr, LANES), lambda i: (i % nt, 0))
        in_specs += [tab] * 6
        args += list(rope_tabs)
    out8 = pl.BlockSpec((tr, w8), lambda i: (i, 0))
    out4 = pl.BlockSpec((tr, w4), lambda i: (i, 0))
    out_specs = [out8, out8, out8, out8, out4, out4]
    out_shape = [jax.ShapeDtypeStruct((m, w), BF16) for w in (w8, w8, w8, w8, w4, w4)]
    if keep_f32:
        out_specs += [out8, out4]
        out_shape += [jax.ShapeDtypeStruct((m, w8), F32), jax.ShapeDtypeStruct((m, w4), F32)]
    return pl.pallas_call(
        functools.partial(_prep_kernel, rope=rope, keep_f32=keep_f32),
        grid=(m // tr,),
        in_specs=in_specs,
        out_specs=out_specs,
        out_shape=out_shape,
        compiler_params=_params("parallel"),
        name="qk_prep",
    )(*args)


def _diff_attn_kernel(*refs, lam_init, n_chunks, ck, has_ctx):
    it = iter(refs)
    lam_ref, subln_ref, q_ref, k_ref, v_ref = (next(it) for _ in range(5))
    if has_ctx:
        kc_ref, vc_ref = next(it), next(it)
    g_ref, o_ref = next(it), next(it)

    tq = q_ref.shape[0]
    q = q_ref[...]
    first = lax.broadcasted_iota(jnp.int32, q.shape, 1) < DIFF_QK_DIM
    zero = jnp.zeros_like(q)
    qs = (jnp.where(first, q, zero), jnp.where(first, zero, q))

    def step(kb, vb, carry):
        out = []
        for c in range(2):
            m, l, acc = carry[c]
            s = _dot_nt(qs[c], kb)
            m_new = jnp.maximum(m, jnp.max(s, axis=-1, keepdims=True))
            alpha = jnp.exp(m - m_new)
            p = jnp.exp(s - m_new)
            l = alpha * l + jnp.sum(p, axis=-1, keepdims=True)
            acc = alpha * acc + _dot(p.astype(BF16), vb)
            out.append((m_new, l, acc))
        return tuple(out)

    init1 = (jnp.full((tq, 1), -jnp.inf, F32), jnp.zeros((tq, 1), F32), jnp.zeros((tq, LANES), F32))
    carry = (init1, init1)

    def body(c, carry):
        r = pl.multiple_of(c * ck, ck)
        return step(k_ref[pl.ds(r, ck), :], v_ref[pl.ds(r, ck), :], carry)

    carry = lax.fori_loop(0, n_chunks, body, carry)
    if has_ctx:
        carry = step(kc_ref[...], vc_ref[...], carry)

    lv = lam_ref[...]
    lam = (jnp.exp(jnp.sum(lv[0:1] * lv[1:2], axis=-1, keepdims=True))
           - jnp.exp(jnp.sum(lv[2:3] * lv[3:4], axis=-1, keepdims=True)) + lam_init)
    (_, l1, a1), (_, l2, a2) = carry
    o = a1 / l1 - lam * (a2 / l2)
    ms = jnp.mean(o * o, axis=-1, keepdims=True)
    y = o * lax.rsqrt(ms + NORM_EPS) * subln_ref[...]
    y = y * (1.0 - lam_init)
    o_ref[...] = (y * _silu(g_ref[...])).astype(BF16)


def _diff_attn(q, k, v, ctx, proj3, lam_vecs, subln, lam_init):
    b, t, _ = q.shape
    tq = _tile(t, 256)
    ck = _tile(t, 512)
    has_ctx = ctx is not None
    qspec = pl.BlockSpec((None, tq, LANES), lambda bi, h, i: (bi, i, h))
    kvspec = pl.BlockSpec((None, t, LANES), lambda bi, h, i: (bi, 0, h))
    in_specs = [pl.BlockSpec((4, DIFF_QK_DIM), lambda bi, h, i: (0, 0)),
                pl.BlockSpec((1, LANES), lambda bi, h, i: (0, 0)),
                qspec, kvspec, kvspec]
    args = [lam_vecs, subln, q, k, v]
    if has_ctx:
        p = ctx[0].shape[1]
        cspec = pl.BlockSpec((None, p, LANES), lambda bi, h, i: (bi, 0, h))
        in_specs += [cspec, cspec]
        args += list(ctx)
    in_specs.append(pl.BlockSpec((None, tq, LANES), lambda bi, h, i: (bi, i, OFF_DG // LANES + h)))
    args.append(proj3)
    return pl.pallas_call(
        functools.partial(_diff_attn_kernel, lam_init=lam_init, n_chunks=t // ck, ck=ck, has_ctx=has_ctx),
        grid=(b, N_HEADS, t // tq),
        in_specs=in_specs,
        out_specs=qspec,
        out_shape=jax.ShapeDtypeStruct((b, t, BRANCH_W), BF16),
        compiler_params=_params("parallel", "parallel", "parallel"),
        name="diff_attn",
    )(*args)


def _win_attn_kernel(*refs, banded, n_blocks, has_ctx):
    it = iter(refs)
    sink_ref, q_ref, k_ref, v_ref = (next(it) for _ in range(4))
    if has_ctx:
        kc_ref, vc_ref = next(it), next(it)
    g_ref, o_ref = next(it), next(it)

    kvh = pl.program_id(1)
    n = pl.program_id(2)
    w = WINDOW
    if banded:
        row = lax.broadcasted_iota(jnp.int32, (w, w), 0)
        col = lax.broadcasted_iota(jnp.int32, (w, w), 1)
        r_prev = pl.multiple_of(jnp.maximum(n - 1, 0) * w, w)
        r_cur = pl.multiple_of(n * w, w)
        r_next = pl.multiple_of(jnp.minimum(n + 1, n_blocks - 1) * w, w)
        keep_prev = (col >= row) & (n > 0)
        keep_next = (col <= row) & (n < n_blocks - 1)
        segs = [(k_ref[pl.ds(r_prev, w), :], v_ref[pl.ds(r_prev, w), :], keep_prev),
                (k_ref[pl.ds(r_cur, w), :], v_ref[pl.ds(r_cur, w), :], None),
                (k_ref[pl.ds(r_next, w), :], v_ref[pl.ds(r_next, w), :], keep_next)]
    else:
        segs = [(k_ref[...], v_ref[...], None)]
    if has_ctx:
        segs.append((kc_ref[...], vc_ref[...], None))

    for j in range(WIN_GROUP):
        sl = slice(j * LANES, (j + 1) * LANES)
        qh = q_ref[:, sl]
        sink = jnp.full((w, 1), sink_ref[kvh * WIN_GROUP + j], F32)
        scores = []
        m = sink
        for kb, _, keep in segs:
            s = _dot_nt(qh, kb)
            if keep is not None:
                s = jnp.where(keep, s, -jnp.inf)
            scores.append(s)
            m = jnp.maximum(m, jnp.max(s, axis=-1, keepdims=True))
        l = jnp.exp(sink - m)
        acc = jnp.zeros((w, LANES), F32)
        for s, (_, vb, _) in zip(scores, segs):
            p = jnp.exp(s - m)
            l = l + jnp.sum(p, axis=-1, keepdims=True)
            acc = acc + _dot(p.astype(BF16), vb)
        o_ref[:, sl] = ((acc / l) * _silu(g_ref[:, sl])).astype(BF16)


def _win_attn(q, k, v, ctx, proj3, sink, banded):
    b, t, _ = q.shape
    nb = t // WINDOW
    gw = WIN_GROUP * LANES
    has_ctx = ctx is not None
    qspec = pl.BlockSpec((None, WINDOW, gw), lambda bi, h, i: (bi, i, h))
    kvspec = pl.BlockSpec((None, t, LANES), lambda bi, h, i: (bi, 0, h))
    in_specs = [pl.BlockSpec(memory_space=pltpu.SMEM), qspec, kvspec, kvspec]
    args = [sink, q, k, v]
    if has_ctx:
        p = ctx[0].shape[1]
        cspec = pl.BlockSpec((None, p, LANES), lambda bi, h, i: (bi, 0, h))
        in_specs += [cspec, cspec]
        args += list(ctx)
    in_specs.append(pl.BlockSpec((None, WINDOW, gw), lambda bi, h, i: (bi, i, OFF_WG // gw + h)))
    args.append(proj3)
    return pl.pallas_call(
        functools.partial(_win_attn_kernel, banded=banded, n_blocks=nb, has_ctx=has_ctx),
        grid=(b, WIN_KV_HEADS, nb),
        in_specs=in_specs,
        out_specs=qspec,
        out_shape=jax.ShapeDtypeStruct((b, t, BRANCH_W), BF16),
        compiler_params=_params("parallel", "parallel", "parallel"),
        name="win_attn",
    )(*args)


def _hgrn_tables(reverse):
    c = HG_CHUNK
    t = np.arange(c)[:, None]
    u = np.arange(c)[None, :]
    mats, masks = [], []
    for m in range(HG_LEVELS):
        half, blk = 1 << m, 2 << m
        r = t - t % blk + half - 1
        right = (t % blk) >= half
        if not reverse:
            e_q = (u >= r + 1) & (u <= t)
            e_k = (u >= t + 1) & (u <= r)
            mats.append(np.where(right, e_q, e_k))
            masks.append(right & ~right.T & (t // blk == u // blk))
        else:
            e_q = (u >= t) & (u <= r)
            e_k = (u >= r + 1) & (u <= t - 1)
            mats.append(np.where(right, e_k, e_q))
            masks.append(~right & right.T & (t // blk == u // blk))
    if not reverse:
        mats += [u <= t, u >= t + 1]
    else:
        mats += [u >= t, u <= t - 1]
    masks.append(t == u)
    return (np.concatenate(mats, 0).astype(np.float32), np.stack(masks).astype(np.float32))


def _hgrn_kernel(lb_ref, s0_ref, mf_ref, mb_ref, kf_ref, kb_ref,
                 qf_ref, vf_ref, zf_ref, qb_ref, vb_ref, zb_ref,
                 of_ref, ob_ref, sout_ref, st_ref, *, layer, n_chunks):
    i = pl.program_id(2)
    c = HG_CHUNK
    pair = 2

    @pl.when(i == 0)
    def _():
        for d in range(2):
            for hh in range(pair):
                st_ref[d, hh] = s0_ref[d, hh].T

    def lower_bound(d):
        rows = [lb_ref[d, r:r + 1, :] for r in range(lb_ref.shape[1])]
        mx = functools.reduce(jnp.maximum, rows)
        es = [jnp.exp(r - mx) for r in rows]
        den = functools.reduce(lambda a, b_: a + b_, es)
        ps = [e / den for e in es]
        return functools.reduce(lambda a, b_: a + b_, ps[:layer + 1]) - ps[0]

    bit_row = lax.broadcasted_iota(jnp.int32, (c, LANES), 0)

    def one_chunk(d, row0, q_ref, v_ref, z_ref, o_ref, m_ref, k_ref, lb):
        rows = pl.ds(row0, c)
        z = z_ref[rows, :]
        f = lb + (1.0 - lb) * jax.nn.sigmoid(z)
        logf = jnp.log(jnp.maximum(f, F_FLOOR))
        kk = (1.0 - lb) * jax.nn.sigmoid(-z)
        mat = m_ref[...]
        e = None
        for part in _split_bf16(logf, 3):
            t_ = _dot(mat, part)
            e = t_ if e is None else e + t_
        ex = jnp.exp(e)
        last = (c - 1) if d == 0 else 0
        for hh in range(pair):
            sl = slice(hh * LANES, (hh + 1) * LANES)
            q = q_ref[rows, sl]
            v = v_ref[rows, sl].astype(BF16)
            k = kk[:, sl]
            exh = ex[:, sl]
            a = k_ref[HG_LEVELS] * _dot_nt(q.astype(BF16), k.astype(BF16))
            for m in range(HG_LEVELS):
                is_q = ((bit_row >> m) & 1) == (1 - d)
                zz = (jnp.where(is_q, q, k) * exh[m * c:(m + 1) * c]).astype(BF16)
                a = a + k_ref[m] * _dot_nt(zz, zz)
            base = HG_LEVELS * c
            qe = (q * exh[base:base + c]).astype(BF16)
            ke = (k * exh[base + c:base + 2 * c]).astype(BF16)
            g = exh[base + last:base + last + 1]
            st = st_ref[d, hh]
            o_ref[rows, sl] = _dot_nt(qe, st.astype(BF16)) + _dot(a.astype(BF16), v)
            st_ref[d, hh] = st * g + lax.dot_general(v, ke, TN_DIMS, preferred_element_type=F32)

    lb_f = lower_bound(0)
    lb_b = lower_bound(1)

    def body(ci, _):
        one_chunk(0, pl.multiple_of(ci * c, c), qf_ref, vf_ref, zf_ref, of_ref, mf_ref, kf_ref, lb_f)
        one_chunk(1, pl.multiple_of((n_chunks - 1 - ci) * c, c), qb_ref, vb_ref, zb_ref, ob_ref,
                  mb_ref, kb_ref, lb_b)
        return 0

    lax.fori_loop(0, n_chunks, body, 0)

    @pl.when(i == pl.num_programs(2) - 1)
    def _():
        for d in range(2):
            for hh in range(pair):
                sout_ref[d, hh] = st_ref[d, hh].T


def _hgrn(proj3, hg_lb, state0, layer):
    b, t, _ = proj3.shape
    tb = _tile(t, 512)
    nt = t // tb
    pw = 2 * LANES
    tabs = [_hgrn_tables(False), _hgrn_tables(True)]
    mats = [jnp.asarray(tb_[0], BF16) for tb_ in tabs]
    masks = [jnp.asarray(tb_[1], F32) for tb_ in tabs]

    def fwd(off):
        return pl.BlockSpec((None, tb, pw), lambda bi, hp, i: (bi, i, off // pw + hp))

    def bwd(off):
        return pl.BlockSpec((None, tb, pw), lambda bi, hp, i: (bi, nt - 1 - i, off // pw + hp))

    whole2 = lambda a: pl.BlockSpec(a.shape, lambda bi, hp, i: (0, 0))
    whole3 = lambda a: pl.BlockSpec(a.shape, lambda bi, hp, i: (0, 0, 0))
    st_spec = pl.BlockSpec((None, 2, 2, LANES, LANES), lambda bi, hp, i: (bi, 0, hp, 0, 0))
    return pl.pallas_call(
        functools.partial(_hgrn_kernel, layer=layer, n_chunks=tb // HG_CHUNK),
        grid=(b, N_HEADS // 2, nt),
        in_specs=[
            pl.BlockSpec((2, hg_lb.shape[1], pw), lambda bi, hp, i: (0, 0, hp)),
            st_spec,
            whole2(mats[0]), whole2(mats[1]), whole3(masks[0]), whole3(masks[1]),
            fwd(OFF_HQ), fwd(OFF_HI), fwd(OFF_HFF),
            bwd(OFF_HQ), bwd(OFF_HI), bwd(OFF_HFB),
        ],
        out_specs=[
            pl.BlockSpec((None, tb, pw), lambda bi, hp, i: (bi, i, hp)),
            pl.BlockSpec((None, tb, pw), lambda bi, hp, i: (bi, nt - 1 - i, hp)),
            st_spec,
        ],
        out_shape=[
            jax.ShapeDtypeStruct((b, t, BRANCH_W), F32),
            jax.ShapeDtypeStruct((b, t, BRANCH_W), F32),
            jax.ShapeDtypeStruct(state0.shape, F32),
        ],
        scratch_shapes=[pltpu.VMEM((2, 2, LANES, LANES), F32)],
        compiler_params=_params("parallel", "parallel", "arbitrary"),
        name="hgrn_scan",
    )(hg_lb, state0, mats[0], mats[1], masks[0], masks[1],
      proj3, proj3, proj3, proj3, proj3, proj3)


def _merge_kernel(yd_ref, of_ref, ob_ref, yw_ref, hg_ref, gain_ref, mg0_ref, mg1_ref, mg2_ref, w_ref,
                  o_ref, yh_ref):
    @pl.when(pl.program_id(1) == 0)
    def _():
        for j in range(N_HEADS):
            sl = slice(j * LANES, (j + 1) * LANES)
            y = _head_norm(of_ref[:, sl] + ob_ref[:, sl], gain_ref[...], LANES)
            yh_ref[:, sl] = (y * _silu(hg_ref[:, sl])).astype(BF16)

    acc = jax.nn.sigmoid(mg0_ref[...]) * _dot(yd_ref[...], w_ref[0])
    acc = acc + jax.nn.sigmoid(mg1_ref[...]) * _dot(yh_ref[...], w_ref[1])
    acc = acc + jax.nn.sigmoid(mg2_ref[...]) * _dot(yw_ref[...], w_ref[2])
    o_ref[...] = acc.astype(BF16)


def _merge(yd, o_f, o_b, yw, proj, hg_gain, w_branch, layer):
    m = yd.shape[0]
    d = w_branch.shape[-1]
    tm = _tile(m, 512)
    tn = _tile(d, 512)
    row = lambda width: pl.BlockSpec((tm, width), lambda i, j: (i, 0))

    def mg(nb):
        return pl.BlockSpec((tm, tn), lambda i, j: (i, (OFF_MG + nb * d) // tn + j))

    return pl.pallas_call(
        _merge_kernel,
        grid=(m // tm, d // tn),
        in_specs=[row(BRANCH_W), row(BRANCH_W), row(BRANCH_W), row(BRANCH_W),
                  pl.BlockSpec((tm, BRANCH_W), lambda i, j: (i, OFF_HG // BRANCH_W)),
                  pl.BlockSpec((1, LANES), lambda i, j: (0, 0)),
                  mg(0), mg(1), mg(2),
                  pl.BlockSpec((None, 3, BRANCH_W, tn), lambda i, j: (layer, 0, 0, j))],
        out_specs=pl.BlockSpec((tm, tn), lambda i, j: (i, j)),
        out_shape=jax.ShapeDtypeStruct((m, d), BF16),
        scratch_shapes=[pltpu.VMEM((tm, BRANCH_W), BF16)],
        compiler_params=_params("parallel", "arbitrary"),
        name="branch_merge",
    )(yd, o_f, o_b, yw, proj, hg_gain, proj, proj, proj, w_branch)


def _out_proj_kernel(x_ref, gate_ref, m_ref, w_ref, o_ref):
    o_ref[...] = x_ref[...] + gate_ref[...] * _dot(m_ref[...], w_ref[...])


def _out_proj(x2d, gate, merged, w_out, layer, rows_per_cond):
    m, d = x2d.shape
    tm = _tile(rows_per_cond, 1024)
    tn = _tile(d, 512)
    return pl.pallas_call(
        _out_proj_kernel,
        grid=(m // tm, d // tn),
        in_specs=[
            pl.BlockSpec((tm, tn), lambda i, j: (i, j)),
            pl.BlockSpec((None, 1, tn), lambda i, j: (i * tm // rows_per_cond, 0, j)),
            pl.BlockSpec((tm, d), lambda i, j: (i, 0)),
            pl.BlockSpec((None, d, tn), lambda i, j: (layer, 0, j)),
        ],
        out_specs=pl.BlockSpec((tm, tn), lambda i, j: (i, j)),
        out_shape=jax.ShapeDtypeStruct((m, d), F32),
        compiler_params=_params("parallel", "parallel"),
        name="out_proj",
    )(x2d, gate, merged, w_out)


def _rope_tables(t, d):
    rows = t // GRID_W
    row = jnp.repeat(jnp.arange(rows, dtype=F32), GRID_W)
    col = jnp.tile(jnp.arange(GRID_W, dtype=F32), rows)
    quarter = d // 4
    inv = ROPE_BASE ** (-jnp.arange(quarter, dtype=F32) / quarter)
    ar = row[:, None] * inv
    ac = col[:, None] * inv
    ang = jnp.concatenate([ar, ar, ac, ac], axis=-1)
    cos, sin = jnp.cos(ang), jnp.sin(ang)
    first = (jnp.arange(d) % (2 * quarter)) < quarter
    sin_up = jnp.where(first, -sin, 0.0)
    sin_dn = jnp.where(first, 0.0, sin)
    rep = LANES // d
    return tuple(jnp.tile(a, (1, rep)) for a in (cos, sin_up, sin_dn))


def _tile_gain(g):
    return jnp.tile(g, LANES // g.shape[0]).reshape(1, LANES)


def _layer(x, mod, l, w, rope_tabs, caches, state0, is_context):
    b, t, d = x.shape
    m = b * t
    rows_per_cond = m // mod.shape[0]
    x2d = x.reshape(m, d)
    shift, scale, gate = (mod[:, None, j * d:(j + 1) * d] for j in range(3))
    proj = _in_proj(x2d, shift, scale, w["norm_gain"][l].reshape(1, d), w["w_in"], l, rows_per_cond)
    gains = (_tile_gain(w["diff_q_norm"][l]), _tile_gain(w["diff_k_norm"][l]),
             _tile_gain(w["win_q_norm"][l]), _tile_gain(w["win_k_norm"][l]))
    prepped = _prep(proj, gains, rope_tabs, t, keep_f32=is_context)
    dq, dk, dv, wq, wk, wv = (a.reshape(b, t, -1) for a in prepped[:6])
    proj3 = proj.reshape(b, t, -1)
    lam_init = 0.8 - 0.6 * math.exp(-0.3 * l)
    if is_context:
        dctx = wctx = None
    else:
        ck_d, cv_d, ck_w, cv_w = caches
        dctx = (ck_d.reshape(b, -1, BRANCH_W).astype(BF16), cv_d.reshape(b, -1, BRANCH_W).astype(BF16))
        wctx = (ck_w.reshape(b, -1, WIN_KV_HEADS * LANES).astype(BF16),
                cv_w.reshape(b, -1, WIN_KV_HEADS * LANES).astype(BF16))
    yd = _diff_attn(dq, dk, dv, dctx, proj3, w["diff_lambda"][l], _tile_gain(w["diff_subln"][l]), lam_init)
    o_f, o_b, s_out = _hgrn(proj3, w["hg_lb"], state0, l)
    yw = _win_attn(wq, wk, wv, wctx, proj3, w["win_sink"][l], banded=not is_context)
    merged = _merge(yd.reshape(m, -1), o_f.reshape(m, -1), o_b.reshape(m, -1), yw.reshape(m, -1), proj,
                    _tile_gain(w["hg_out_norm"][l]), w["w_branch"], l)
    y = _out_proj(x2d, gate, merged, w["w_out"], l, rows_per_cond).reshape(b, t, d)
    if not is_context:
        return y, None
    new = (prepped[6].reshape(b, t, N_HEADS, 2, DIFF_QK_DIM),
           proj3[:, :, OFF_DV:OFF_DV + BRANCH_W].reshape(b, t, N_HEADS, LANES),
           prepped[7].reshape(b, t, WIN_KV_HEADS, LANES),
           proj3[:, :, OFF_WV:OFF_WV + WIN_KV_HEADS * LANES].reshape(b, t, WIN_KV_HEADS, LANES),
           s_out)
    return y, new


def kernel(x_prompt, x_sample, cache_diff_k, cache_diff_v, cache_win_k, cache_win_v, state_hgrn, c, c_ctx, norm_gain, w_ada, b_ada, w_in, diff_q_norm, diff_k_norm, diff_lambda, diff_subln, hg_lb, hg_out_norm, win_q_norm, win_k_norm, win_sink, w_branch, w_out):
    depth = w_in.shape[0]
    dec_b, dec_t, d = x_sample.shape
    w = {"norm_gain": norm_gain, "w_in": w_in.astype(BF16), "diff_q_norm": diff_q_norm,
         "diff_k_norm": diff_k_norm, "diff_lambda": diff_lambda, "diff_subln": diff_subln,
         "hg_lb": hg_lb, "hg_out_norm": hg_out_norm, "win_q_norm": win_q_norm, "win_k_norm": win_k_norm,
         "win_sink": win_sink, "w_branch": w_branch.astype(BF16), "w_out": w_out.astype(BF16)}

    cond_rows = 8 * ((1 + dec_b + 7) // 8)
    cond = jnp.zeros((cond_rows, d), F32).at[0].set(c_ctx).at[1:1 + dec_b].set(c)
    mod = _ada(cond, w_ada, b_ada)

    y_prompt = x_prompt
    news = []
    zero_state = jnp.zeros((x_prompt.shape[0],) + state_hgrn.shape[2:], F32)
    for l in range(depth):
        y_prompt, new = _layer(y_prompt, mod[l, 0:1], l, w, None, None, zero_state, True)
        news.append(new)

    ropes = _rope_tables(dec_t, DIFF_QK_DIM) + _rope_tables(dec_t, LANES)
    y_sample = x_sample
    for l in range(depth):
        caches = (cache_diff_k[:, l], cache_diff_v[:, l], cache_win_k[:, l], cache_win_v[:, l])
        y_sample, _ = _layer(y_sample, mod[l, 1:1 + dec_b], l, w, ropes, caches, state_hgrn[:, l], False)

    stacked = tuple(jnp.stack([n[j] for n in news], axis=1) for j in range(5))
    return (y_prompt, y_sample) + stacked
```

```python
import functools
import math

import numpy as np
import jax
import jax.numpy as jnp
from jax import lax
from jax.experimental import pallas as pl
from jax.experimental.pallas import tpu as pltpu

F32 = jnp.float32
BF16 = jnp.bfloat16

NORM_EPS = 1e-6
F_FLOOR = 1e-30
ROPE_BASE = 10000.0
GRID_W = 64
WINDOW = 128

LANES = 128
N_HEADS = 8
WIN_KV_HEADS = 4
WIN_GROUP = 2
DIFF_QK_DIM = 64
BRANCH_W = N_HEADS * LANES
HG_CHUNK = 64
HG_LEVELS = 6
HG_MXU_LEVELS = 3
DIFF_KV_CHUNK = 512
DIFF_Q_TILE = 2048
LOG2E = math.log2(math.e)

OFF_DQ, OFF_DK, OFF_DV, OFF_DG = 0, 1024, 2048, 3072
OFF_HQ, OFF_HFF, OFF_HFB, OFF_HI, OFF_HG = 4096, 5120, 6144, 7168, 8192
OFF_WQ, OFF_WK, OFF_WV, OFF_WG = 9216, 10240, 10752, 11264
OFF_MG = 12288

VMEM_LIMIT = 48 * 1024 * 1024

NT_DIMS = (((1,), (1,)), ((), ()))
TN_DIMS = (((0,), (0,)), ((), ()))


def _params(*sem):
    return pltpu.CompilerParams(dimension_semantics=sem, vmem_limit_bytes=VMEM_LIMIT)


def _tile(n, preferred):
    t = min(n, preferred)
    while n % t:
        t //= 2
    return t


def _dot(a, b):
    return jnp.dot(a, b, preferred_element_type=F32)


def _dot_nt(a, b):
    return lax.dot_general(a, b, NT_DIMS, preferred_element_type=F32)


def _silu(x):
    return x * jax.nn.sigmoid(x)


def _split_bf16(x, terms):
    out = []
    for _ in range(terms - 1):
        hi = x.astype(BF16)
        out.append(hi)
        x = x - hi.astype(F32)
    out.append(x.astype(BF16))
    return out


def _ada_kernel(c_ref, w_ref, b_ref, o_ref):
    a = _silu(c_ref[...])
    a_hi, a_lo = _split_bf16(a, 2)
    w_hi, w_lo = _split_bf16(w_ref[...], 2)
    acc = _dot(a_hi, w_hi) + _dot(a_lo, w_hi) + _dot(a_hi, w_lo)
    o_ref[...] = acc + b_ref[...]


def _ada(cond, w_ada, b_ada):
    depth, d, n = w_ada.shape
    tn = _tile(n, 512)
    rows = cond.shape[0]
    return pl.pallas_call(
        _ada_kernel,
        grid=(depth, n // tn),
        in_specs=[
            pl.BlockSpec((rows, d), lambda l, j: (0, 0)),
            pl.BlockSpec((None, d, tn), lambda l, j: (l, 0, j)),
            pl.BlockSpec((None, 1, tn), lambda l, j: (l, 0, j)),
        ],
        out_specs=pl.BlockSpec((None, rows, tn), lambda l, j: (l, 0, j)),
        out_shape=jax.ShapeDtypeStruct((depth, rows, n), F32),
        compiler_params=_params("parallel", "parallel"),
        name="ada_mod",
    )(cond, w_ada, b_ada.reshape(depth, 1, n))


def _in_proj_kernel(x_ref, shift_ref, scale_ref, gain_ref, w_ref, o_ref, h_ref):
    @pl.when(pl.program_id(1) == 0)
    def _():
        x = x_ref[...]
        ms = jnp.mean(x * x, axis=-1, keepdims=True)
        y = x * lax.rsqrt(ms + NORM_EPS) * gain_ref[...]
        h_ref[...] = (y * (1.0 + scale_ref[...]) + shift_ref[...]).astype(BF16)

    o_ref[...] = _dot(h_ref[...], w_ref[...])


def _in_proj(x2d, shift, scale, gain, w_in, layer, rows_per_cond):
    m, d = x2d.shape
    n = w_in.shape[-1]
    tm = _tile(rows_per_cond, 1024)
    tn = _tile(n, 512)
    return pl.pallas_call(
        _in_proj_kernel,
        grid=(m // tm, n // tn),
        in_specs=[
            pl.BlockSpec((tm, d), lambda i, j: (i, 0)),
            pl.BlockSpec((None, 1, d), lambda i, j: (i * tm // rows_per_cond, 0, 0)),
            pl.BlockSpec((None, 1, d), lambda i, j: (i * tm // rows_per_cond, 0, 0)),
            pl.BlockSpec((1, d), lambda i, j: (0, 0)),
            pl.BlockSpec((None, d, tn), lambda i, j: (layer, 0, j)),
        ],
        out_specs=pl.BlockSpec((tm, tn), lambda i, j: (i, j)),
        out_shape=jax.ShapeDtypeStruct((m, n), F32),
        scratch_shapes=[pltpu.VMEM((tm, d), BF16)],
        compiler_params=_params("parallel", "arbitrary"),
        name="in_proj",
    )(x2d, shift, scale, gain, w_in)


def _head_norm(x, gain, group):
    x2 = x * x
    if group == LANES:
        ms = jnp.sum(x2, axis=-1, keepdims=True) * (1.0 / LANES)
    else:
        lo = lax.broadcasted_iota(jnp.int32, x.shape, 1) < group
        s_lo = jnp.sum(jnp.where(lo, x2, 0.0), axis=-1, keepdims=True)
        s_hi = jnp.sum(jnp.where(lo, 0.0, x2), axis=-1, keepdims=True)
        ms = jnp.where(lo, s_lo, s_hi) * (1.0 / group)
    return x * lax.rsqrt(ms + NORM_EPS) * gain


def _rope(y, cos, sin_up, sin_dn, quarter):
    return (y * cos + pltpu.roll(y, LANES - quarter, 1) * sin_up
            + pltpu.roll(y, quarter, 1) * sin_dn)


def _prep_kernel(*refs, rope, keep_f32):
    it = iter(refs)
    dq_ref, dk_ref, dv_ref, wq_ref, wk_ref, wv_ref = (next(it) for _ in range(6))
    gdq_ref, gdk_ref, gwq_ref, gwk_ref = (next(it) for _ in range(4))
    if rope:
        cd_ref, ud_ref, nd_ref, cw_ref, uw_ref, nw_ref = (next(it) for _ in range(6))
    odq, odk, odv, owq, owk, owv = (next(it) for _ in range(6))
    if keep_f32:
        odk32, owk32 = (next(it) for _ in range(2))

    def slab(src, j, gain_ref, group, quarter, tabs):
        y = _head_norm(src[:, j * LANES:(j + 1) * LANES], gain_ref[...], group)
        if rope:
            y = _rope(y, tabs[0][...], tabs[1][...], tabs[2][...], quarter)
        return y

    dtabs = (cd_ref, ud_ref, nd_ref) if rope else None
    wtabs = (cw_ref, uw_ref, nw_ref) if rope else None
    for j in range(N_HEADS):
        sl = slice(j * LANES, (j + 1) * LANES)
        q = slab(dq_ref, j, gdq_ref, DIFF_QK_DIM, DIFF_QK_DIM // 4, dtabs)
        odq[:, sl] = (q * (DIFF_QK_DIM ** -0.5 * LOG2E)).astype(BF16)
        odv[j] = dv_ref[:, sl].T.astype(BF16)
        k = slab(dk_ref, j, gdk_ref, DIFF_QK_DIM, DIFF_QK_DIM // 4, dtabs)
        odk[:, sl] = k.astype(BF16)
        if keep_f32:
            odk32[:, sl] = k
        q = slab(wq_ref, j, gwq_ref, LANES, LANES // 4, wtabs)
        owq[:, sl] = (q * (LANES ** -0.5)).astype(BF16)
    for j in range(WIN_KV_HEADS):
        sl = slice(j * LANES, (j + 1) * LANES)
        k = slab(wk_ref, j, gwk_ref, LANES, LANES // 4, wtabs)
        owk[:, sl] = k.astype(BF16)
        if keep_f32:
            owk32[:, sl] = k
    owv[...] = wv_ref[...].astype(BF16)


def _prep(proj, gains, rope_tabs, seq_len, keep_f32):
    m = proj.shape[0]
    tr = _tile(seq_len, DIFF_KV_CHUNK)
    nt = seq_len // tr
    w8, w4 = BRANCH_W, WIN_KV_HEADS * LANES
    rope = rope_tabs is not None

    def col(width, off):
        return pl.BlockSpec((tr, width), lambda i: (i, off // width))

    small = pl.BlockSpec((1, LANES), lambda i: (0, 0))
    in_specs = [col(w8, OFF_DQ), col(w8, OFF_DK), col(w8, OFF_DV), col(w8, OFF_WQ),
                col(w4, OFF_WK), col(w4, OFF_WV), small, small, small, small]
    args = [proj] * 6 + list(gains)
    if rope:
        tab = pl.BlockSpec((tr, LANES), lambda i: (i % nt, 0))
        in_specs += [tab] * 6
        args += list(rope_tabs)
    out8 = pl.BlockSpec((tr, w8), lambda i: (i, 0))
    out4 = pl.BlockSpec((tr, w4), lambda i: (i, 0))
    out_vt = pl.BlockSpec((None, N_HEADS, None, LANES, tr), lambda i: (i // nt, 0, i % nt, 0, 0))
    out_specs = [out8, out8, out_vt, out8, out4, out4]
    out_shape = [jax.ShapeDtypeStruct((m, w8), BF16), jax.ShapeDtypeStruct((m, w8), BF16),
                 jax.ShapeDtypeStruct((m // seq_len, N_HEADS, nt, LANES, tr), BF16),
                 jax.ShapeDtypeStruct((m, w8), BF16), jax.ShapeDtypeStruct((m, w4), BF16),
                 jax.ShapeDtypeStruct((m, w4), BF16)]
    if keep_f32:
        out_specs += [out8, out4]
        out_shape += [jax.ShapeDtypeStruct((m, w8), F32), jax.ShapeDtypeStruct((m, w4), F32)]
    return pl.pallas_call(
        functools.partial(_prep_kernel, rope=rope, keep_f32=keep_f32),
        grid=(m // tr,),
        in_specs=in_specs,
        out_specs=out_specs,
        out_shape=out_shape,
        compiler_params=_params("parallel"),
        name="qk_prep",
    )(*args)


def _diff_attn_kernel(*refs, lam_init, n_chunks, ck, has_ctx):
    it = iter(refs)
    lam_ref, subln_ref, q_ref, k_ref, vt_ref = (next(it) for _ in range(5))
    if has_ctx:
        kc_ref, vct_ref = next(it), next(it)
    g_ref, o_ref, acc_ref = next(it), next(it), next(it)

    tq = q_ref.shape[0]
    q = q_ref[...]
    first = lax.broadcasted_iota(jnp.int32, q.shape, 1) < DIFF_QK_DIM
    zero = jnp.zeros_like(q)
    q2 = jnp.concatenate([jnp.where(first, q, zero), jnp.where(first, zero, q)], axis=0)

    def step(kb, vtb, carry, is_first):
        m, l = carry
        s = _dot_nt(kb, q2)
        m_new = jnp.maximum(m, jnp.max(s, axis=0, keepdims=True))
        p = jnp.exp2(s - m_new)
        pv = _dot(vtb, p.astype(BF16))
        if is_first:
            l = jnp.sum(p, axis=0, keepdims=True)
            acc_ref[...] = pv
        else:
            alpha = jnp.exp2(m - m_new)
            l = alpha * l + jnp.sum(p, axis=0, keepdims=True)
            acc_ref[...] = alpha * acc_ref[...] + pv
        return m_new, l

    none = jnp.zeros((1, 2 * tq), F32)
    carry = step(k_ref[pl.ds(0, ck), :], vt_ref[0], (none, none), True)

    def body(c, carry):
        r = pl.multiple_of(c * ck, ck)
        return step(k_ref[pl.ds(r, ck), :], vt_ref[c], carry, False)

    carry = lax.fori_loop(1, n_chunks, body, carry)
    if has_ctx:
        carry = step(kc_ref[...], vct_ref[...], carry, False)

    lv = lam_ref[...]
    lam = (jnp.exp(jnp.sum(lv[0:1] * lv[1:2], axis=-1, keepdims=True))
           - jnp.exp(jnp.sum(lv[2:3] * lv[3:4], axis=-1, keepdims=True)) + lam_init)
    r = acc_ref[...] / carry[1]
    o = (r[:, :tq] - lam * r[:, tq:]).T
    ms = jnp.mean(o * o, axis=-1, keepdims=True)
    y = o * lax.rsqrt(ms + NORM_EPS) * subln_ref[...]
    y = y * (1.0 - lam_init)
    o_ref[...] = (y * _silu(g_ref[...])).astype(BF16)


def _diff_attn(q, k, vt, ctx, proj3, lam_vecs, subln, lam_init):
    b, t, _ = q.shape
    n_chunks, ck = vt.shape[2], vt.shape[4]
    tq = _tile(t, DIFF_Q_TILE)
    has_ctx = ctx is not None
    qspec = pl.BlockSpec((None, tq, LANES), lambda bi, h, i: (bi, i, h))
    in_specs = [pl.BlockSpec((4, DIFF_QK_DIM), lambda bi, h, i: (0, 0)),
                pl.BlockSpec((1, LANES), lambda bi, h, i: (0, 0)),
                qspec,
                pl.BlockSpec((None, t, LANES), lambda bi, h, i: (bi, 0, h)),
                pl.BlockSpec((None, None, n_chunks, LANES, ck), lambda bi, h, i: (bi, h, 0, 0, 0))]
    args = [lam_vecs, subln, q, k, vt]
    if has_ctx:
        p = ctx[0].shape[1]
        in_specs += [pl.BlockSpec((None, p, LANES), lambda bi, h, i: (bi, 0, h)),
                     pl.BlockSpec((None, None, LANES, p), lambda bi, h, i: (bi, h, 0, 0))]
        args += list(ctx)
    in_specs.append(pl.BlockSpec((None, tq, LANES), lambda bi, h, i: (bi, i, OFF_DG // LANES + h)))
    args.append(proj3)
    return pl.pallas_call(
        functools.partial(_diff_attn_kernel, lam_init=lam_init, n_chunks=n_chunks, ck=ck, has_ctx=has_ctx),
        grid=(b, N_HEADS, t // tq),
        in_specs=in_specs,
        out_specs=qspec,
        out_shape=jax.ShapeDtypeStruct((b, t, BRANCH_W), BF16),
        scratch_shapes=[pltpu.VMEM((LANES, 2 * tq), F32)],
        compiler_params=_params("parallel", "parallel", "parallel"),
        name="diff_attn",
    )(*args)


def _win_attn_kernel(*refs, banded, n_blocks, has_ctx):
    it = iter(refs)
    sink_ref, q_ref, k_ref, v_ref = (next(it) for _ in range(4))
    if has_ctx:
        kc_ref, vc_ref = next(it), next(it)
    g_ref, o_ref = next(it), next(it)

    kvh = pl.program_id(1)
    n = pl.program_id(2)
    w = WINDOW
    if banded:
        row = lax.broadcasted_iota(jnp.int32, (w, w), 0)
        col = lax.broadcasted_iota(jnp.int32, (w, w), 1)
        r_prev = pl.multiple_of(jnp.maximum(n - 1, 0) * w, w)
        r_cur = pl.multiple_of(n * w, w)
        r_next = pl.multiple_of(jnp.minimum(n + 1, n_blocks - 1) * w, w)
        keep_prev = (col >= row) & (n > 0)
        keep_next = (col <= row) & (n < n_blocks - 1)
        segs = [(k_ref[pl.ds(r_prev, w), :], v_ref[pl.ds(r_prev, w), :], keep_prev),
                (k_ref[pl.ds(r_cur, w), :], v_ref[pl.ds(r_cur, w), :], None),
                (k_ref[pl.ds(r_next, w), :], v_ref[pl.ds(r_next, w), :], keep_next)]
    else:
        segs = [(k_ref[...], v_ref[...], None)]
    if has_ctx:
        segs.append((kc_ref[...], vc_ref[...], None))

    for j in range(WIN_GROUP):
        sl = slice(j * LANES, (j + 1) * LANES)
        qh = q_ref[:, sl]
        sink = jnp.full((w, 1), sink_ref[kvh * WIN_GROUP + j], F32)
        scores = []
        m = sink
        for kb, _, keep in segs:
            s = _dot_nt(qh, kb)
            if keep is not None:
                s = jnp.where(keep, s, -jnp.inf)
            scores.append(s)
            m = jnp.maximum(m, jnp.max(s, axis=-1, keepdims=True))
        l = jnp.exp(sink - m)
        acc = jnp.zeros((w, LANES), F32)
        for s, (_, vb, _) in zip(scores, segs):
            p = jnp.exp(s - m)
            l = l + jnp.sum(p, axis=-1, keepdims=True)
            acc = acc + _dot(p.astype(BF16), vb)
        o_ref[:, sl] = ((acc / l) * _silu(g_ref[:, sl])).astype(BF16)


def _win_attn(q, k, v, ctx, proj3, sink, banded):
    b, t, _ = q.shape
    nb = t // WINDOW
    gw = WIN_GROUP * LANES
    has_ctx = ctx is not None
    qspec = pl.BlockSpec((None, WINDOW, gw), lambda bi, h, i: (bi, i, h))
    kvspec = pl.BlockSpec((None, t, LANES), lambda bi, h, i: (bi, 0, h))
    in_specs = [pl.BlockSpec(memory_space=pltpu.SMEM), qspec, kvspec, kvspec]
    args = [sink, q, k, v]
    if has_ctx:
        p = ctx[0].shape[1]
        cspec = pl.BlockSpec((None, p, LANES), lambda bi, h, i: (bi, 0, h))
        in_specs += [cspec, cspec]
        args += list(ctx)
    in_specs.append(pl.BlockSpec((None, WINDOW, gw), lambda bi, h, i: (bi, i, OFF_WG // gw + h)))
    args.append(proj3)
    return pl.pallas_call(
        functools.partial(_win_attn_kernel, banded=banded, n_blocks=nb, has_ctx=has_ctx),
        grid=(b, WIN_KV_HEADS, nb),
        in_specs=in_specs,
        out_specs=qspec,
        out_shape=jax.ShapeDtypeStruct((b, t, BRANCH_W), BF16),
        compiler_params=_params("parallel", "parallel", "parallel"),
        name="win_attn",
    )(*args)


def _hgrn_tables(reverse):
    c = HG_CHUNK
    t = np.arange(c)[:, None]
    u = np.arange(c)[None, :]
    mats, masks = [], []
    for m in range(HG_LEVELS):
        half, blk = 1 << m, 2 << m
        r = t - t % blk + half - 1
        right = (t % blk) >= half
        if not reverse:
            e_q = (u >= r + 1) & (u <= t)
            e_k = (u >= t + 1) & (u <= r)
            mats.append(np.where(right, e_q, e_k))
            masks.append(right & ~right.T & (t // blk == u // blk))
        else:
            e_q = (u >= t) & (u <= r)
            e_k = (u >= r + 1) & (u <= t - 1)
            mats.append(np.where(right, e_k, e_q))
            masks.append(~right & right.T & (t // blk == u // blk))
    mats = mats[:HG_MXU_LEVELS] + [(u <= t) if not reverse else (u >= t)]
    masks.append(t == u)
    return (np.concatenate(mats, 0).astype(np.float32), np.stack(masks).astype(np.float32))


def _hgrn_constants():
    c = HG_CHUNK
    tabs = [_hgrn_tables(False), _hgrn_tables(True)]
    mats = np.stack([np.tile(tb[0], (1, 3)) for tb in tabs])
    masks = np.zeros((HG_LEVELS + 1, 4, c, LANES), np.float32)
    for a in range(4):
        lo = (a % 2) * c
        masks[:, a, :, lo:lo + c] = tabs[a // 2][1]
    return mats, masks


def _hgrn_kernel(lb_ref, s0_ref, mat_ref, mask_ref,
                 qf_ref, vf_ref, zf_ref, qb_ref, vb_ref, zb_ref,
                 of_ref, ob_ref, sout_ref, st_ref, *, layer, n_chunks):
    i = pl.program_id(2)
    c = HG_CHUNK
    pair = 2
    pw = pair * LANES

    @pl.when(i == 0)
    def _():
        for d in range(2):
            for hh in range(pair):
                st_ref[2 * d + hh] = s0_ref[d, hh]

    def lower_bound(d):
        rows = [lb_ref[d, r:r + 1, :] for r in range(lb_ref.shape[1])]
        mx = functools.reduce(jnp.maximum, rows)
        es = [jnp.exp(r - mx) for r in rows]
        den = functools.reduce(lambda a, b_: a + b_, es)
        ps = [e / den for e in es]
        return functools.reduce(lambda a, b_: a + b_, ps[:layer + 1]) - ps[0]

    bit_row = lax.broadcasted_iota(jnp.int32, (c, pw), 0)

    def decays(d, z, lb):
        f = lb + (1.0 - lb) * jax.nn.sigmoid(z)
        logf = jnp.log(jnp.maximum(f, F_FLOOR))
        kk = (1.0 - lb) * jax.nn.sigmoid(-z)
        parts = jnp.concatenate(_split_bf16(logf, 3), axis=0)
        e_small = _dot(mat_ref[d], parts)
        run = e_small[HG_MXU_LEVELS * c:]
        levels = [jnp.exp(e_small[m * c:(m + 1) * c]) for m in range(HG_MXU_LEVELS)]
        for m in range(HG_MXU_LEVELS, HG_LEVELS):
            half, blk = 1 << m, 2 << m
            edge = half - 1 + d
            rho = jnp.concatenate(
                [jnp.broadcast_to(run[b0 + edge:b0 + edge + 1], (blk, pw)) for b0 in range(0, c, blk)], axis=0)
            is_q = ((bit_row >> m) & 1) == (1 - d)
            levels.append(jnp.exp(jnp.where(is_q, run - rho, rho - run)))
        last = (c - 1) if d == 0 else 0
        total = run[last:last + 1]
        return kk, levels, jnp.exp(run), jnp.exp(total - run), jnp.exp(total)

    lbs = (lower_bound(0), lower_bound(1))

    def body(ci, _):
        rows = (pl.ds(pl.multiple_of(ci * c, c), c), pl.ds(pl.multiple_of((n_chunks - 1 - ci) * c, c), c))
        qs = (qf_ref[rows[0], :], qb_ref[rows[1], :])
        vs = (vf_ref[rows[0], :].astype(BF16), vb_ref[rows[1], :].astype(BF16))
        zs = (zf_ref[rows[0], :], zb_ref[rows[1], :])
        dec = [decays(d, zs[d], lbs[d]) for d in range(2)]

        def stacked(fn):
            wide = [fn(d).astype(BF16) for d in range(2)]
            return jnp.concatenate([wide[a // 2][:, (a % 2) * LANES:(a % 2 + 1) * LANES] for a in range(4)],
                                   axis=0)

        def pair_weights(g, m):
            return [mask_ref[m, a] * g[a * c:(a + 1) * c, (a // 2) * LANES:(a // 2 + 1) * LANES]
                    for a in range(4)]

        acc = pair_weights(_dot_nt(stacked(lambda d: qs[d]), stacked(lambda d: dec[d][0])), HG_LEVELS)
        for m in range(HG_LEVELS):
            def level_rows(d, m=m):
                is_q = ((bit_row >> m) & 1) == (1 - d)
                return jnp.where(is_q, qs[d], dec[d][0]) * dec[d][1][m]
            lm = stacked(level_rows)
            acc = [x + y for x, y in zip(acc, pair_weights(_dot_nt(lm, lm), m))]

        outs = (of_ref, ob_ref)
        for a in range(4):
            d, hh = a // 2, a % 2
            sl = slice(hh * LANES, (hh + 1) * LANES)
            kk, _, from_start, to_end, whole = dec[d]
            v = vs[d][:, sl]
            qe = (qs[d][:, sl] * from_start[:, sl]).astype(BF16)
            ke = (kk[:, sl] * to_end[:, sl]).astype(BF16)
            st = st_ref[a]
            lhs = jnp.concatenate([acc[a].astype(BF16), qe], axis=1)
            rhs = jnp.concatenate([v, v, st.astype(BF16)], axis=0)
            outs[d][rows[d], sl] = _dot(lhs, rhs)
            decay_col = jnp.broadcast_to(whole[:, sl], (LANES, LANES)).T
            st_ref[a] = st * decay_col + lax.dot_general(ke, v, TN_DIMS, preferred_element_type=F32)
        return 0

    lax.fori_loop(0, n_chunks, body, 0)

    @pl.when(i == pl.num_programs(2) - 1)
    def _():
        for d in range(2):
            for hh in range(pair):
                sout_ref[d, hh] = st_ref[2 * d + hh]


def _hgrn(proj3, hg_lb, state0, layer):
    b, t, _ = proj3.shape
    tb = _tile(t, 512)
    nt = t // tb
    pw = 2 * LANES
    mats, masks = _hgrn_constants()
    mats = jnp.asarray(mats, BF16)
    masks = jnp.asarray(masks, F32)

    def fwd(off):
        return pl.BlockSpec((None, tb, pw), lambda bi, hp, i: (bi, i, off // pw + hp))

    def bwd(off):
        return pl.BlockSpec((None, tb, pw), lambda bi, hp, i: (bi, nt - 1 - i, off // pw + hp))

    st_spec = pl.BlockSpec((None, 2, 2, LANES, LANES), lambda bi, hp, i: (bi, 0, hp, 0, 0))
    return pl.pallas_call(
        functools.partial(_hgrn_kernel, layer=layer, n_chunks=tb // HG_CHUNK),
        grid=(b, N_HEADS // 2, nt),
        in_specs=[
            pl.BlockSpec((2, hg_lb.shape[1], pw), lambda bi, hp, i: (0, 0, hp)),
            st_spec,
            pl.BlockSpec(mats.shape, lambda bi, hp, i: (0, 0, 0)),
            pl.BlockSpec(masks.shape, lambda bi, hp, i: (0, 0, 0, 0)),
            fwd(OFF_HQ), fwd(OFF_HI), fwd(OFF_HFF),
            bwd(OFF_HQ), bwd(OFF_HI), bwd(OFF_HFB),
        ],
        out_specs=[
            pl.BlockSpec((None, tb, pw), lambda bi, hp, i: (bi, i, hp)),
            pl.BlockSpec((None, tb, pw), lambda bi, hp, i: (bi, nt - 1 - i, hp)),
            st_spec,
        ],
        out_shape=[
            jax.ShapeDtypeStruct((b, t, BRANCH_W), F32),
            jax.ShapeDtypeStruct((b, t, BRANCH_W), F32),
            jax.ShapeDtypeStruct(state0.shape, F32),
        ],
        scratch_shapes=[pltpu.VMEM((4, LANES, LANES), F32)],
        compiler_params=_params("parallel", "parallel", "arbitrary"),
        name="hgrn_scan",
    )(hg_lb, state0, mats, masks, proj3, proj3, proj3, proj3, proj3, proj3)


def _merge_kernel(yd_ref, of_ref, ob_ref, yw_ref, hg_ref, gain_ref, mg0_ref, mg1_ref, mg2_ref, w_ref,
                  o_ref, yh_ref):
    @pl.when(pl.program_id(1) == 0)
    def _():
        for j in range(N_HEADS):
            sl = slice(j * LANES, (j + 1) * LANES)
            y = _head_norm(of_ref[:, sl] + ob_ref[:, sl], gain_ref[...], LANES)
            yh_ref[:, sl] = (y * _silu(hg_ref[:, sl])).astype(BF16)

    acc = jax.nn.sigmoid(mg0_ref[...]) * _dot(yd_ref[...], w_ref[0])
    acc = acc + jax.nn.sigmoid(mg1_ref[...]) * _dot(yh_ref[...], w_ref[1])
    acc = acc + jax.nn.sigmoid(mg2_ref[...]) * _dot(yw_ref[...], w_ref[2])
    o_ref[...] = acc.astype(BF16)


def _merge(yd, o_f, o_b, yw, proj, hg_gain, w_branch, layer):
    m = yd.shape[0]
    d = w_branch.shape[-1]
    tm = _tile(m, 512)
    tn = _tile(d, 512)
    row = lambda width: pl.BlockSpec((tm, width), lambda i, j: (i, 0))

    def mg(nb):
        return pl.BlockSpec((tm, tn), lambda i, j: (i, (OFF_MG + nb * d) // tn + j))

    return pl.pallas_call(
        _merge_kernel,
        grid=(m // tm, d // tn),
        in_specs=[row(BRANCH_W), row(BRANCH_W), row(BRANCH_W), row(BRANCH_W),
                  pl.BlockSpec((tm, BRANCH_W), lambda i, j: (i, OFF_HG // BRANCH_W)),
                  pl.BlockSpec((1, LANES), lambda i, j: (0, 0)),
                  mg(0), mg(1), mg(2),
                  pl.BlockSpec((None, 3, BRANCH_W, tn), lambda i, j: (layer, 0, 0, j))],
        out_specs=pl.BlockSpec((tm, tn), lambda i, j: (i, j)),
        out_shape=jax.ShapeDtypeStruct((m, d), BF16),
        scratch_shapes=[pltpu.VMEM((tm, BRANCH_W), BF16)],
        compiler_params=_params("parallel", "arbitrary"),
        name="branch_merge",
    )(yd, o_f, o_b, yw, proj, hg_gain, proj, proj, proj, w_branch)


def _out_proj_kernel(x_ref, gate_ref, m_ref, w_ref, o_ref):
    o_ref[...] = x_ref[...] + gate_ref[...] * _dot(m_ref[...], w_ref[...])


def _out_proj(x2d, gate, merged, w_out, layer, rows_per_cond):
    m, d = x2d.shape
    tm = _tile(rows_per_cond, 1024)
    tn = _tile(d, 512)
    return pl.pallas_call(
        _out_proj_kernel,
        grid=(m // tm, d // tn),
        in_specs=[
            pl.BlockSpec((tm, tn), lambda i, j: (i, j)),
            pl.BlockSpec((None, 1, tn), lambda i, j: (i * tm // rows_per_cond, 0, j)),
            pl.BlockSpec((tm, d), lambda i, j: (i, 0)),
            pl.BlockSpec((None, d, tn), lambda i, j: (layer, 0, j)),
        ],
        out_specs=pl.BlockSpec((tm, tn), lambda i, j: (i, j)),
        out_shape=jax.ShapeDtypeStruct((m, d), F32),
        compiler_params=_params("parallel", "parallel"),
        name="out_proj",
    )(x2d, gate, merged, w_out)


def _rope_tables(t, d):
    rows = t // GRID_W
    row = jnp.repeat(jnp.arange(rows, dtype=F32), GRID_W)
    col = jnp.tile(jnp.arange(GRID_W, dtype=F32), rows)
    quarter = d // 4
    inv = ROPE_BASE ** (-jnp.arange(quarter, dtype=F32) / quarter)
    ar = row[:, None] * inv
    ac = col[:, None] * inv
    ang = jnp.concatenate([ar, ar, ac, ac], axis=-1)
    cos, sin = jnp.cos(ang), jnp.sin(ang)
    first = (jnp.arange(d) % (2 * quarter)) < quarter
    sin_up = jnp.where(first, -sin, 0.0)
    sin_dn = jnp.where(first, 0.0, sin)
    rep = LANES // d
    return tuple(jnp.tile(a, (1, rep)) for a in (cos, sin_up, sin_dn))


def _tile_gain(g):
    return jnp.tile(g, LANES // g.shape[0]).reshape(1, LANES)


def _layer(x, mod, l, w, rope_tabs, caches, state0, is_context):
    b, t, d = x.shape
    m = b * t
    rows_per_cond = m // mod.shape[0]
    x2d = x.reshape(m, d)
    shift, scale, gate = (mod[:, None, j * d:(j + 1) * d] for j in range(3))
    proj = _in_proj(x2d, shift, scale, w["norm_gain"][l].reshape(1, d), w["w_in"], l, rows_per_cond)
    gains = (_tile_gain(w["diff_q_norm"][l]), _tile_gain(w["diff_k_norm"][l]),
             _tile_gain(w["win_q_norm"][l]), _tile_gain(w["win_k_norm"][l]))
    prepped = _prep(proj, gains, rope_tabs, t, keep_f32=is_context)
    dq, dk, wq, wk, wv = (prepped[j].reshape(b, t, -1) for j in (0, 1, 3, 4, 5))
    dv = prepped[2]
    proj3 = proj.reshape(b, t, -1)
    lam_init = 0.8 - 0.6 * math.exp(-0.3 * l)
    if is_context:
        dctx = wctx = None
    else:
        ck_d, cv_d, ck_w, cv_w = caches
        dctx = (ck_d.reshape(b, -1, BRANCH_W).astype(BF16),
                jnp.transpose(cv_d, (0, 2, 3, 1)).astype(BF16))
        wctx = (ck_w.reshape(b, -1, WIN_KV_HEADS * LANES).astype(BF16),
                cv_w.reshape(b, -1, WIN_KV_HEADS * LANES).astype(BF16))
    yd = _diff_attn(dq, dk, dv, dctx, proj3, w["diff_lambda"][l], _tile_gain(w["diff_subln"][l]), lam_init)
    o_f, o_b, s_out = _hgrn(proj3, w["hg_lb"], state0, l)
    yw = _win_attn(wq, wk, wv, wctx, proj3, w["win_sink"][l], banded=not is_context)
    merged = _merge(yd.reshape(m, -1), o_f.reshape(m, -1), o_b.reshape(m, -1), yw.reshape(m, -1), proj,
                    _tile_gain(w["hg_out_norm"][l]), w["w_branch"], l)
    y = _out_proj(x2d, gate, merged, w["w_out"], l, rows_per_cond).reshape(b, t, d)
    if not is_context:
        return y, None
    new = (prepped[6].reshape(b, t, N_HEADS, 2, DIFF_QK_DIM),
           proj3[:, :, OFF_DV:OFF_DV + BRANCH_W].reshape(b, t, N_HEADS, LANES),
           prepped[7].reshape(b, t, WIN_KV_HEADS, LANES),
           proj3[:, :, OFF_WV:OFF_WV + WIN_KV_HEADS * LANES].reshape(b, t, WIN_KV_HEADS, LANES),
           s_out)
    return y, new


def kernel(x_prompt, x_sample, cache_diff_k, cache_diff_v, cache_win_k, cache_win_v, state_hgrn, c, c_ctx, norm_gain, w_ada, b_ada, w_in, diff_q_norm, diff_k_norm, diff_lambda, diff_subln, hg_lb, hg_out_norm, win_q_norm, win_k_norm, win_sink, w_branch, w_out):
    depth = w_in.shape[0]
    dec_b, dec_t, d = x_sample.shape
    w = {"norm_gain": norm_gain, "w_in": w_in.astype(BF16), "diff_q_norm": diff_q_norm,
         "diff_k_norm": diff_k_norm, "diff_lambda": diff_lambda, "diff_subln": diff_subln,
         "hg_lb": hg_lb, "hg_out_norm": hg_out_norm, "win_q_norm": win_q_norm, "win_k_norm": win_k_norm,
         "win_sink": win_sink, "w_branch": w_branch.astype(BF16), "w_out": w_out.astype(BF16)}

    cond_rows = 8 * ((1 + dec_b + 7) // 8)
    cond = jnp.zeros((cond_rows, d), F32).at[0].set(c_ctx).at[1:1 + dec_b].set(c)
    mod = _ada(cond, w_ada, b_ada)

    y_prompt = x_prompt
    news = []
    zero_state = jnp.zeros((x_prompt.shape[0],) + state_hgrn.shape[2:], F32)
    for l in range(depth):
        y_prompt, new = _layer(y_prompt, mod[l, 0:1], l, w, None, None, zero_state, True)
        news.append(new)

    ropes = _rope_tables(dec_t, DIFF_QK_DIM) + _rope_tables(dec_t, LANES)
    y_sample = x_sample
    for l in range(depth):
        caches = (cache_diff_k[:, l], cache_diff_v[:, l], cache_win_k[:, l], cache_win_v[:, l])
        y_sample, _ = _layer(y_sample, mod[l, 1:1 + dec_b], l, w, ropes, caches, state_hgrn[:, l], False)

    stacked = tuple(jnp.stack([n[j] for n in news], axis=1) for j in range(5))
    return (y_prompt, y_sample) + stacked
```

```python
import functools
import math

import numpy as np
import jax
import jax.numpy as jnp
from jax import lax
from jax.experimental import pallas as pl
from jax.experimental.pallas import tpu as pltpu

F32 = jnp.float32
BF16 = jnp.bfloat16

NORM_EPS = 1e-6
F_FLOOR = 1e-30
ROPE_BASE = 10000.0
GRID_W = 64
WINDOW = 128

LANES = 128
N_HEADS = 8
WIN_KV_HEADS = 4
WIN_GROUP = 2
DIFF_QK_DIM = 64
BRANCH_W = N_HEADS * LANES
HG_CHUNK = 64
HG_LEVELS = 6
HG_MXU_LEVELS = 3
DIFF_KV_CHUNK = 512
WIN_BLOCKS_PER_STEP = 8
DIFF_Q_TILE = 2048
LOG2E = math.log2(math.e)

OFF_DQ, OFF_DK, OFF_DV, OFF_DG = 0, 1024, 2048, 3072
OFF_HQ, OFF_HFF, OFF_HFB, OFF_HI, OFF_HG = 4096, 5120, 6144, 7168, 8192
OFF_WQ, OFF_WK, OFF_WV, OFF_WG = 9216, 10240, 10752, 11264
OFF_MG = 12288

VMEM_LIMIT = 48 * 1024 * 1024

NT_DIMS = (((1,), (1,)), ((), ()))
TN_DIMS = (((0,), (0,)), ((), ()))


def _params(*sem):
    return pltpu.CompilerParams(dimension_semantics=sem, vmem_limit_bytes=VMEM_LIMIT)


def _tile(n, preferred):
    t = min(n, preferred)
    while n % t:
        t //= 2
    return t


def _dot(a, b):
    return jnp.dot(a, b, preferred_element_type=F32)


def _dot_nt(a, b):
    return lax.dot_general(a, b, NT_DIMS, preferred_element_type=F32)


def _sigmoid(x):
    return 0.5 * jnp.tanh(0.5 * x) + 0.5


def _silu(x):
    return x * _sigmoid(x)


def _split_bf16(x, terms):
    out = []
    for _ in range(terms - 1):
        hi = x.astype(BF16)
        out.append(hi)
        x = x - hi.astype(F32)
    out.append(x.astype(BF16))
    return out


def _ada_kernel(c_ref, w_ref, b_ref, o_ref):
    a = _silu(c_ref[...])
    a_hi, a_lo = _split_bf16(a, 2)
    w_hi, w_lo = _split_bf16(w_ref[...], 2)
    acc = _dot(a_hi, w_hi) + _dot(a_lo, w_hi) + _dot(a_hi, w_lo)
    o_ref[...] = acc + b_ref[...]


def _ada(cond, w_ada, b_ada):
    depth, d, n = w_ada.shape
    tn = _tile(n, 512)
    rows = cond.shape[0]
    return pl.pallas_call(
        _ada_kernel,
        grid=(depth, n // tn),
        in_specs=[
            pl.BlockSpec((rows, d), lambda l, j: (0, 0)),
            pl.BlockSpec((None, d, tn), lambda l, j: (l, 0, j)),
            pl.BlockSpec((None, 1, tn), lambda l, j: (l, 0, j)),
        ],
        out_specs=pl.BlockSpec((None, rows, tn), lambda l, j: (l, 0, j)),
        out_shape=jax.ShapeDtypeStruct((depth, rows, n), F32),
        compiler_params=_params("parallel", "parallel"),
        name="ada_mod",
    )(cond, w_ada, b_ada.reshape(depth, 1, n))


def _in_proj_kernel(x_ref, shift_ref, scale_ref, gain_ref, w_ref, o_ref, h_ref):
    @pl.when(pl.program_id(1) == 0)
    def _():
        x = x_ref[...]
        ms = jnp.mean(x * x, axis=-1, keepdims=True)
        y = x * lax.rsqrt(ms + NORM_EPS) * gain_ref[...]
        h_ref[...] = (y * (1.0 + scale_ref[...]) + shift_ref[...]).astype(BF16)

    o_ref[...] = _dot(h_ref[...], w_ref[...])


def _in_proj(x2d, shift, scale, gain, w_in, layer, rows_per_cond):
    m, d = x2d.shape
    n = w_in.shape[-1]
    tm = _tile(rows_per_cond, 1024)
    tn = _tile(n, 1024)
    return pl.pallas_call(
        _in_proj_kernel,
        grid=(m // tm, n // tn),
        in_specs=[
            pl.BlockSpec((tm, d), lambda i, j: (i, 0)),
            pl.BlockSpec((None, 1, d), lambda i, j: (i * tm // rows_per_cond, 0, 0)),
            pl.BlockSpec((None, 1, d), lambda i, j: (i * tm // rows_per_cond, 0, 0)),
            pl.BlockSpec((1, d), lambda i, j: (0, 0)),
            pl.BlockSpec((None, d, tn), lambda i, j: (layer, 0, j)),
        ],
        out_specs=pl.BlockSpec((tm, tn), lambda i, j: (i, j)),
        out_shape=jax.ShapeDtypeStruct((m, n), F32),
        scratch_shapes=[pltpu.VMEM((tm, d), BF16)],
        compiler_params=_params("parallel", "arbitrary"),
        name="in_proj",
    )(x2d, shift, scale, gain, w_in)


def _head_norm(x, gain, group):
    x2 = x * x
    if group == LANES:
        ms = jnp.sum(x2, axis=-1, keepdims=True) * (1.0 / LANES)
    else:
        lo = lax.broadcasted_iota(jnp.int32, x.shape, 1) < group
        s_lo = jnp.sum(jnp.where(lo, x2, 0.0), axis=-1, keepdims=True)
        s_hi = jnp.sum(jnp.where(lo, 0.0, x2), axis=-1, keepdims=True)
        ms = jnp.where(lo, s_lo, s_hi) * (1.0 / group)
    return x * lax.rsqrt(ms + NORM_EPS) * gain


def _rope(y, cos, sin_up, sin_dn, quarter):
    return (y * cos + pltpu.roll(y, LANES - quarter, 1) * sin_up
            + pltpu.roll(y, quarter, 1) * sin_dn)


def _prep_kernel(*refs, rope, keep_f32):
    it = iter(refs)
    dq_ref, dk_ref, dv_ref, wq_ref, wk_ref, wv_ref = (next(it) for _ in range(6))
    gdq_ref, gdk_ref, gwq_ref, gwk_ref = (next(it) for _ in range(4))
    if rope:
        cd_ref, ud_ref, nd_ref, cw_ref, uw_ref, nw_ref = (next(it) for _ in range(6))
    odq, odk, odv, owq, owk, owv = (next(it) for _ in range(6))
    if keep_f32:
        odk32, owk32 = (next(it) for _ in range(2))

    def slab(src, j, gain_ref, group, quarter, tabs):
        y = _head_norm(src[:, j * LANES:(j + 1) * LANES], gain_ref[...], group)
        if rope:
            y = _rope(y, tabs[0][...], tabs[1][...], tabs[2][...], quarter)
        return y

    dtabs = (cd_ref, ud_ref, nd_ref) if rope else None
    wtabs = (cw_ref, uw_ref, nw_ref) if rope else None
    for j in range(N_HEADS):
        sl = slice(j * LANES, (j + 1) * LANES)
        q = slab(dq_ref, j, gdq_ref, DIFF_QK_DIM, DIFF_QK_DIM // 4, dtabs)
        odq[:, sl] = (q * (DIFF_QK_DIM ** -0.5 * LOG2E)).astype(BF16)
        odv[j] = dv_ref[:, sl].T.astype(BF16)
        k = slab(dk_ref, j, gdk_ref, DIFF_QK_DIM, DIFF_QK_DIM // 4, dtabs)
        odk[:, sl] = k.astype(BF16)
        if keep_f32:
            odk32[:, sl] = k
        q = slab(wq_ref, j, gwq_ref, LANES, LANES // 4, wtabs)
        owq[:, sl] = (q * (LANES ** -0.5 * LOG2E)).astype(BF16)
    for j in range(WIN_KV_HEADS):
        sl = slice(j * LANES, (j + 1) * LANES)
        k = slab(wk_ref, j, gwk_ref, LANES, LANES // 4, wtabs)
        owk[:, sl] = k.astype(BF16)
        if keep_f32:
            owk32[:, sl] = k
    owv[...] = wv_ref[...].astype(BF16)


def _prep(proj, gains, rope_tabs, seq_len, keep_f32):
    m = proj.shape[0]
    tr = _tile(seq_len, DIFF_KV_CHUNK)
    nt = seq_len // tr
    w8, w4 = BRANCH_W, WIN_KV_HEADS * LANES
    rope = rope_tabs is not None

    def col(width, off):
        return pl.BlockSpec((tr, width), lambda i: (i, off // width))

    small = pl.BlockSpec((1, LANES), lambda i: (0, 0))
    in_specs = [col(w8, OFF_DQ), col(w8, OFF_DK), col(w8, OFF_DV), col(w8, OFF_WQ),
                col(w4, OFF_WK), col(w4, OFF_WV), small, small, small, small]
    args = [proj] * 6 + list(gains)
    if rope:
        tab = pl.BlockSpec((tr, LANES), lambda i: (i % nt, 0))
        in_specs += [tab] * 6
        args += list(rope_tabs)
    out8 = pl.BlockSpec((tr, w8), lambda i: (i, 0))
    out4 = pl.BlockSpec((tr, w4), lambda i: (i, 0))
    out_vt = pl.BlockSpec((None, N_HEADS, None, LANES, tr), lambda i: (i // nt, 0, i % nt, 0, 0))
    out_specs = [out8, out8, out_vt, out8, out4, out4]
    out_shape = [jax.ShapeDtypeStruct((m, w8), BF16), jax.ShapeDtypeStruct((m, w8), BF16),
                 jax.ShapeDtypeStruct((m // seq_len, N_HEADS, nt, LANES, tr), BF16),
                 jax.ShapeDtypeStruct((m, w8), BF16), jax.ShapeDtypeStruct((m, w4), BF16),
                 jax.ShapeDtypeStruct((m, w4), BF16)]
    if keep_f32:
        out_specs += [out8, out4]
        out_shape += [jax.ShapeDtypeStruct((m, w8), F32), jax.ShapeDtypeStruct((m, w4), F32)]
    return pl.pallas_call(
        functools.partial(_prep_kernel, rope=rope, keep_f32=keep_f32),
        grid=(m // tr,),
        in_specs=in_specs,
        out_specs=out_specs,
        out_shape=out_shape,
        compiler_params=_params("parallel"),
        name="qk_prep",
    )(*args)


def _diff_attn_kernel(*refs, lam_init, n_chunks, ck, has_ctx):
    it = iter(refs)
    lam_ref, subln_ref, q_ref, k_ref, vt_ref = (next(it) for _ in range(5))
    if has_ctx:
        kc_ref, vct_ref = next(it), next(it)
    g_ref, o_ref, acc_ref = next(it), next(it), next(it)

    tq = q_ref.shape[0]
    q = q_ref[...]
    first = lax.broadcasted_iota(jnp.int32, q.shape, 1) < DIFF_QK_DIM
    zero = jnp.zeros_like(q)
    q2 = jnp.concatenate([jnp.where(first, q, zero), jnp.where(first, zero, q)], axis=0)

    def step(kb, vtb, carry, is_first):
        m, l = carry
        s = _dot_nt(kb, q2)
        m_new = jnp.maximum(m, jnp.max(s, axis=0, keepdims=True))
        p = jnp.exp2(s - m_new)
        pv = _dot(vtb, p.astype(BF16))
        if is_first:
            l = jnp.sum(p, axis=0, keepdims=True)
            acc_ref[...] = pv
        else:
            alpha = jnp.exp2(m - m_new)
            l = alpha * l + jnp.sum(p, axis=0, keepdims=True)
            acc_ref[...] = alpha * acc_ref[...] + pv
        return m_new, l

    none = jnp.zeros((1, 2 * tq), F32)
    carry = step(k_ref[pl.ds(0, ck), :], vt_ref[0], (none, none), True)

    def body(c, carry):
        r = pl.multiple_of(c * ck, ck)
        return step(k_ref[pl.ds(r, ck), :], vt_ref[c], carry, False)

    carry = lax.fori_loop(1, n_chunks, body, carry)
    if has_ctx:
        carry = step(kc_ref[...], vct_ref[...], carry, False)

    lv = lam_ref[...]
    lam = (jnp.exp(jnp.sum(lv[0:1] * lv[1:2], axis=-1, keepdims=True))
           - jnp.exp(jnp.sum(lv[2:3] * lv[3:4], axis=-1, keepdims=True)) + lam_init)
    r = acc_ref[...] / carry[1]
    o = (r[:, :tq] - lam * r[:, tq:]).T
    ms = jnp.mean(o * o, axis=-1, keepdims=True)
    y = o * lax.rsqrt(ms + NORM_EPS) * subln_ref[...]
    y = y * (1.0 - lam_init)
    o_ref[...] = (y * _silu(g_ref[...])).astype(BF16)


def _diff_attn(q, k, vt, ctx, proj3, lam_vecs, subln, lam_init):
    b, t, _ = q.shape
    n_chunks, ck = vt.shape[2], vt.shape[4]
    tq = _tile(t, DIFF_Q_TILE)
    has_ctx = ctx is not None
    qspec = pl.BlockSpec((None, tq, LANES), lambda bi, h, i: (bi, i, h))
    in_specs = [pl.BlockSpec((4, DIFF_QK_DIM), lambda bi, h, i: (0, 0)),
                pl.BlockSpec((1, LANES), lambda bi, h, i: (0, 0)),
                qspec,
                pl.BlockSpec((None, t, LANES), lambda bi, h, i: (bi, 0, h)),
                pl.BlockSpec((None, None, n_chunks, LANES, ck), lambda bi, h, i: (bi, h, 0, 0, 0))]
    args = [lam_vecs, subln, q, k, vt]
    if has_ctx:
        p = ctx[0].shape[1]
        in_specs += [pl.BlockSpec((None, p, LANES), lambda bi, h, i: (bi, 0, h)),
                     pl.BlockSpec((None, None, LANES, p), lambda bi, h, i: (bi, h, 0, 0))]
        args += list(ctx)
    in_specs.append(pl.BlockSpec((None, tq, LANES), lambda bi, h, i: (bi, i, OFF_DG // LANES + h)))
    args.append(proj3)
    return pl.pallas_call(
        functools.partial(_diff_attn_kernel, lam_init=lam_init, n_chunks=n_chunks, ck=ck, has_ctx=has_ctx),
        grid=(b, N_HEADS, t // tq),
        in_specs=in_specs,
        out_specs=qspec,
        out_shape=jax.ShapeDtypeStruct((b, t, BRANCH_W), BF16),
        scratch_shapes=[pltpu.VMEM((LANES, 2 * tq), F32)],
        compiler_params=_params("parallel", "parallel", "parallel"),
        name="diff_attn",
    )(*args)


def _win_attn_kernel(*refs, banded, n_blocks, blocks_per_step, has_ctx):
    it = iter(refs)
    sink_ref, q_ref, k_ref, v_ref = (next(it) for _ in range(4))
    if has_ctx:
        kc_ref, vc_ref = next(it), next(it)
    g_ref, o_ref = next(it), next(it)

    kvh = pl.program_id(1)
    w = WINDOW
    gw = WIN_GROUP * w
    sink = jnp.concatenate([jnp.full((w, 1), sink_ref[kvh * WIN_GROUP + j] * LOG2E, F32)
                            for j in range(WIN_GROUP)], axis=0)
    if banded:
        row = lax.broadcasted_iota(jnp.int32, (gw, w), 0) & (w - 1)
        col = lax.broadcasted_iota(jnp.int32, (gw, w), 1)
    shared = []
    if not banded:
        shared.append((k_ref[...], v_ref[...], None))
    if has_ctx:
        shared.append((kc_ref[...], vc_ref[...], None))

    for u in range(blocks_per_step):
        rows = slice(u * w, (u + 1) * w)
        q2 = jnp.concatenate([q_ref[rows, j * LANES:(j + 1) * LANES] for j in range(WIN_GROUP)], axis=0)
        segs = []
        if banded:
            n = pl.program_id(2) * blocks_per_step + u
            r_prev = pl.multiple_of(jnp.maximum(n - 1, 0) * w, w)
            r_cur = pl.multiple_of(n * w, w)
            r_next = pl.multiple_of(jnp.minimum(n + 1, n_blocks - 1) * w, w)
            keep_prev = (col >= row) & (n > 0)
            keep_next = (col <= row) & (n < n_blocks - 1)
            segs = [(k_ref[pl.ds(r_prev, w), :], v_ref[pl.ds(r_prev, w), :], keep_prev),
                    (k_ref[pl.ds(r_cur, w), :], v_ref[pl.ds(r_cur, w), :], None),
                    (k_ref[pl.ds(r_next, w), :], v_ref[pl.ds(r_next, w), :], keep_next)]
        segs += shared
        scores = []
        for kb, _, keep in segs:
            s = _dot_nt(q2, kb)
            scores.append(s if keep is None else jnp.where(keep, s, -jnp.inf))
        s = jnp.concatenate(scores, axis=1)
        m = jnp.maximum(sink, jnp.max(s, axis=-1, keepdims=True))
        p = jnp.exp2(s - m)
        l = jnp.exp2(sink - m) + jnp.sum(p, axis=-1, keepdims=True)
        v_all = jnp.concatenate([vb for _, vb, _ in segs], axis=0)
        o = _dot(p.astype(BF16), v_all) / l
        for j in range(WIN_GROUP):
            sl = slice(j * LANES, (j + 1) * LANES)
            o_ref[rows, sl] = (o[j * w:(j + 1) * w] * _silu(g_ref[rows, sl])).astype(BF16)


def _win_attn(q, k, v, ctx, proj3, sink, banded):
    b, t, _ = q.shape
    nb = t // WINDOW
    gw = WIN_GROUP * LANES
    has_ctx = ctx is not None
    per_step = _tile(nb, WIN_BLOCKS_PER_STEP)
    tq = per_step * WINDOW
    qspec = pl.BlockSpec((None, tq, gw), lambda bi, h, i: (bi, i, h))
    kvspec = pl.BlockSpec((None, t, LANES), lambda bi, h, i: (bi, 0, h))
    in_specs = [pl.BlockSpec(memory_space=pltpu.SMEM), qspec, kvspec, kvspec]
    args = [sink, q, k, v]
    if has_ctx:
        p = ctx[0].shape[1]
        cspec = pl.BlockSpec((None, p, LANES), lambda bi, h, i: (bi, 0, h))
        in_specs += [cspec, cspec]
        args += list(ctx)
    in_specs.append(pl.BlockSpec((None, tq, gw), lambda bi, h, i: (bi, i, OFF_WG // gw + h)))
    args.append(proj3)
    return pl.pallas_call(
        functools.partial(_win_attn_kernel, banded=banded, n_blocks=nb, blocks_per_step=per_step,
                          has_ctx=has_ctx),
        grid=(b, WIN_KV_HEADS, nb // per_step),
        in_specs=in_specs,
        out_specs=qspec,
        out_shape=jax.ShapeDtypeStruct((b, t, BRANCH_W), BF16),
        compiler_params=_params("parallel", "parallel", "parallel"),
        name="win_attn",
    )(*args)


def _hgrn_tables(reverse):
    c = HG_CHUNK
    t = np.arange(c)[:, None]
    u = np.arange(c)[None, :]
    mats, masks = [], []
    for m in range(HG_LEVELS):
        half, blk = 1 << m, 2 << m
        r = t - t % blk + half - 1
        right = (t % blk) >= half
        if not reverse:
            e_q = (u >= r + 1) & (u <= t)
            e_k = (u >= t + 1) & (u <= r)
            mats.append(np.where(right, e_q, e_k))
            masks.append(right & ~right.T & (t // blk == u // blk))
        else:
            e_q = (u >= t) & (u <= r)
            e_k = (u >= r + 1) & (u <= t - 1)
            mats.append(np.where(right, e_k, e_q))
            masks.append(~right & right.T & (t // blk == u // blk))
    mats = mats[:HG_MXU_LEVELS] + [(u <= t) if not reverse else (u >= t)]
    masks.append(t == u)
    return (np.concatenate(mats, 0).astype(np.float32), np.stack(masks).astype(np.float32))


def _hgrn_constants():
    c = HG_CHUNK
    tabs = [_hgrn_tables(False), _hgrn_tables(True)]
    mats = np.stack([np.tile(tb[0], (1, 3)) for tb in tabs])
    masks = np.zeros((HG_LEVELS + 1, 4, c, LANES), np.float32)
    for a in range(4):
        lo = (a % 2) * c
        masks[:, a, :, lo:lo + c] = tabs[a // 2][1]
    return mats, masks


def _hgrn_kernel(lb_ref, s0_ref, mat_ref, mask_ref,
                 qf_ref, vf_ref, zf_ref, qb_ref, vb_ref, zb_ref,
                 of_ref, ob_ref, sout_ref, st_ref, *, layer, n_chunks):
    i = pl.program_id(2)
    c = HG_CHUNK
    pair = 2
    pw = pair * LANES

    @pl.when(i == 0)
    def _():
        for d in range(2):
            for hh in range(pair):
                st_ref[2 * d + hh] = s0_ref[d, hh]

    def lower_bound(d):
        rows = [lb_ref[d, r:r + 1, :] for r in range(lb_ref.shape[1])]
        mx = functools.reduce(jnp.maximum, rows)
        es = [jnp.exp(r - mx) for r in rows]
        den = functools.reduce(lambda a, b_: a + b_, es)
        ps = [e / den for e in es]
        return functools.reduce(lambda a, b_: a + b_, ps[:layer + 1]) - ps[0]

    bit_row = lax.broadcasted_iota(jnp.int32, (c, pw), 0)

    def side_rows(m, d, on_query_rows, on_key_rows):
        half = 1 << m
        if half % 8:
            return jnp.where(((bit_row >> m) & 1) == (1 - d), on_query_rows, on_key_rows)
        slabs = [(on_query_rows if (r0 // half) % 2 == 1 - d else on_key_rows)[r0:r0 + half]
                 for r0 in range(0, c, half)]
        return jnp.concatenate(slabs, axis=0)

    def decays(d, z, lb):
        e = jnp.exp(-jnp.abs(z))
        r = 1.0 / (1.0 + e)
        er = e * r
        pos = z >= 0.0
        f = lb + (1.0 - lb) * jnp.where(pos, r, er)
        kk = (1.0 - lb) * jnp.where(pos, er, r)
        logf = jnp.log(jnp.maximum(f, F_FLOOR)) * LOG2E
        parts = jnp.concatenate(_split_bf16(logf, 3), axis=0)
        e_small = _dot(mat_ref[d], parts)
        run = e_small[HG_MXU_LEVELS * c:]
        levels = [jnp.exp2(e_small[m * c:(m + 1) * c]) for m in range(HG_MXU_LEVELS)]
        for m in range(HG_MXU_LEVELS, HG_LEVELS):
            half, blk = 1 << m, 2 << m
            edge = half - 1 + d
            rho = jnp.concatenate(
                [jnp.broadcast_to(run[b0 + edge:b0 + edge + 1], (blk, pw)) for b0 in range(0, c, blk)], axis=0)
            diff = run - rho
            levels.append(jnp.exp2(side_rows(m, d, diff, -diff)))
        last = (c - 1) if d == 0 else 0
        total = run[last:last + 1]
        return kk, levels, jnp.exp2(run), jnp.exp2(total - run), jnp.exp2(total)

    lbs = (lower_bound(0), lower_bound(1))

    def body(ci, _):
        rows = (pl.ds(pl.multiple_of(ci * c, c), c), pl.ds(pl.multiple_of((n_chunks - 1 - ci) * c, c), c))
        qs = (qf_ref[rows[0], :], qb_ref[rows[1], :])
        vs = (vf_ref[rows[0], :].astype(BF16), vb_ref[rows[1], :].astype(BF16))
        zs = (zf_ref[rows[0], :], zb_ref[rows[1], :])
        dec = [decays(d, zs[d], lbs[d]) for d in range(2)]

        def stacked(fn):
            wide = [fn(d).astype(BF16) for d in range(2)]
            return jnp.concatenate([wide[a // 2][:, (a % 2) * LANES:(a % 2 + 1) * LANES] for a in range(4)],
                                   axis=0)

        def pair_weights(g, m):
            return [mask_ref[m, a] * g[a * c:(a + 1) * c, (a // 2) * LANES:(a // 2 + 1) * LANES]
                    for a in range(4)]

        acc = pair_weights(_dot_nt(stacked(lambda d: qs[d]), stacked(lambda d: dec[d][0])), HG_LEVELS)
        for m in range(HG_LEVELS):
            def level_rows(d, m=m):
                return side_rows(m, d, qs[d], dec[d][0]) * dec[d][1][m]
            lm = stacked(level_rows)
            acc = [x + y for x, y in zip(acc, pair_weights(_dot_nt(lm, lm), m))]

        outs = (of_ref, ob_ref)
        for a in range(4):
            d, hh = a // 2, a % 2
            sl = slice(hh * LANES, (hh + 1) * LANES)
            kk, _, from_start, to_end, whole = dec[d]
            v = vs[d][:, sl]
            qe = (qs[d][:, sl] * from_start[:, sl]).astype(BF16)
            ke = (kk[:, sl] * to_end[:, sl]).astype(BF16)
            st = st_ref[a]
            lhs = jnp.concatenate([acc[a].astype(BF16), qe], axis=1)
            rhs = jnp.concatenate([v, v, st.astype(BF16)], axis=0)
            outs[d][rows[d], sl] = _dot(lhs, rhs)
            decay_col = jnp.broadcast_to(whole[:, sl], (LANES, LANES)).T
            st_ref[a] = st * decay_col + lax.dot_general(ke, v, TN_DIMS, preferred_element_type=F32)
        return 0

    lax.fori_loop(0, n_chunks, body, 0, unroll=4)

    @pl.when(i == pl.num_programs(2) - 1)
    def _():
        for d in range(2):
            for hh in range(pair):
                sout_ref[d, hh] = st_ref[2 * d + hh]


def _hgrn(proj3, hg_lb, state0, layer):
    b, t, _ = proj3.shape
    tb = _tile(t, 512)
    nt = t // tb
    pw = 2 * LANES
    mats, masks = _hgrn_constants()
    mats = jnp.asarray(mats, BF16)
    masks = jnp.asarray(masks, F32)

    def fwd(off):
        return pl.BlockSpec((None, tb, pw), lambda bi, hp, i: (bi, i, off // pw + hp))

    def bwd(off):
        return pl.BlockSpec((None, tb, pw), lambda bi, hp, i: (bi, nt - 1 - i, off // pw + hp))

    st_spec = pl.BlockSpec((None, 2, 2, LANES, LANES), lambda bi, hp, i: (bi, 0, hp, 0, 0))
    return pl.pallas_call(
        functools.partial(_hgrn_kernel, layer=layer, n_chunks=tb // HG_CHUNK),
        grid=(b, N_HEADS // 2, nt),
        in_specs=[
            pl.BlockSpec((2, hg_lb.shape[1], pw), lambda bi, hp, i: (0, 0, hp)),
            st_spec,
            pl.BlockSpec(mats.shape, lambda bi, hp, i: (0, 0, 0)),
            pl.BlockSpec(masks.shape, lambda bi, hp, i: (0, 0, 0, 0)),
            fwd(OFF_HQ), fwd(OFF_HI), fwd(OFF_HFF),
            bwd(OFF_HQ), bwd(OFF_HI), bwd(OFF_HFB),
        ],
        out_specs=[
            pl.BlockSpec((None, tb, pw), lambda bi, hp, i: (bi, i, hp)),
            pl.BlockSpec((None, tb, pw), lambda bi, hp, i: (bi, nt - 1 - i, hp)),
            st_spec,
        ],
        out_shape=[
            jax.ShapeDtypeStruct((b, t, BRANCH_W), F32),
            jax.ShapeDtypeStruct((b, t, BRANCH_W), F32),
            jax.ShapeDtypeStruct(state0.shape, F32),
        ],
        scratch_shapes=[pltpu.VMEM((4, LANES, LANES), F32)],
        compiler_params=_params("parallel", "parallel", "arbitrary"),
        name="hgrn_scan",
    )(hg_lb, state0, mats, masks, proj3, proj3, proj3, proj3, proj3, proj3)


def _merge_kernel(yd_ref, of_ref, ob_ref, yw_ref, hg_ref, gain_ref, mg0_ref, mg1_ref, mg2_ref, w_ref,
                  o_ref, yh_ref):
    @pl.when(pl.program_id(1) == 0)
    def _():
        for j in range(N_HEADS):
            sl = slice(j * LANES, (j + 1) * LANES)
            y = _head_norm(of_ref[:, sl] + ob_ref[:, sl], gain_ref[...], LANES)
            yh_ref[:, sl] = (y * _silu(hg_ref[:, sl])).astype(BF16)

    acc = _sigmoid(mg0_ref[...]) * _dot(yd_ref[...], w_ref[0])
    acc = acc + _sigmoid(mg1_ref[...]) * _dot(yh_ref[...], w_ref[1])
    acc = acc + _sigmoid(mg2_ref[...]) * _dot(yw_ref[...], w_ref[2])
    o_ref[...] = acc.astype(BF16)


def _merge(yd, o_f, o_b, yw, proj, hg_gain, w_branch, layer):
    m = yd.shape[0]
    d = w_branch.shape[-1]
    tm = _tile(m, 512)
    tn = _tile(d, 512)
    row = lambda width: pl.BlockSpec((tm, width), lambda i, j: (i, 0))

    def mg(nb):
        return pl.BlockSpec((tm, tn), lambda i, j: (i, (OFF_MG + nb * d) // tn + j))

    return pl.pallas_call(
        _merge_kernel,
        grid=(m // tm, d // tn),
        in_specs=[row(BRANCH_W), row(BRANCH_W), row(BRANCH_W), row(BRANCH_W),
                  pl.BlockSpec((tm, BRANCH_W), lambda i, j: (i, OFF_HG // BRANCH_W)),
                  pl.BlockSpec((1, LANES), lambda i, j: (0, 0)),
                  mg(0), mg(1), mg(2),
                  pl.BlockSpec((None, 3, BRANCH_W, tn), lambda i, j: (layer, 0, 0, j))],
        out_specs=pl.BlockSpec((tm, tn), lambda i, j: (i, j)),
        out_shape=jax.ShapeDtypeStruct((m, d), BF16),
        scratch_shapes=[pltpu.VMEM((tm, BRANCH_W), BF16)],
        compiler_params=_params("parallel", "arbitrary"),
        name="branch_merge",
    )(yd, o_f, o_b, yw, proj, hg_gain, proj, proj, proj, w_branch)


def _out_proj_kernel(x_ref, gate_ref, m_ref, w_ref, o_ref):
    o_ref[...] = x_ref[...] + gate_ref[...] * _dot(m_ref[...], w_ref[...])


def _out_proj(x2d, gate, merged, w_out, layer, rows_per_cond):
    m, d = x2d.shape
    tm = _tile(rows_per_cond, 1024)
    tn = _tile(d, 512)
    return pl.pallas_call(
        _out_proj_kernel,
        grid=(m // tm, d // tn),
        in_specs=[
            pl.BlockSpec((tm, tn), lambda i, j: (i, j)),
            pl.BlockSpec((None, 1, tn), lambda i, j: (i * tm // rows_per_cond, 0, j)),
            pl.BlockSpec((tm, d), lambda i, j: (i, 0)),
            pl.BlockSpec((None, d, tn), lambda i, j: (layer, 0, j)),
        ],
        out_specs=pl.BlockSpec((tm, tn), lambda i, j: (i, j)),
        out_shape=jax.ShapeDtypeStruct((m, d), F32),
        compiler_params=_params("parallel", "parallel"),
        name="out_proj",
    )(x2d, gate, merged, w_out)


def _rope_tables(t, d):
    rows = t // GRID_W
    row = jnp.repeat(jnp.arange(rows, dtype=F32), GRID_W)
    col = jnp.tile(jnp.arange(GRID_W, dtype=F32), rows)
    quarter = d // 4
    inv = ROPE_BASE ** (-jnp.arange(quarter, dtype=F32) / quarter)
    ar = row[:, None] * inv
    ac = col[:, None] * inv
    ang = jnp.concatenate([ar, ar, ac, ac], axis=-1)
    cos, sin = jnp.cos(ang), jnp.sin(ang)
    first = (jnp.arange(d) % (2 * quarter)) < quarter
    sin_up = jnp.where(first, -sin, 0.0)
    sin_dn = jnp.where(first, 0.0, sin)
    rep = LANES // d
    return tuple(jnp.tile(a, (1, rep)) for a in (cos, sin_up, sin_dn))


def _tile_gain(g):
    return jnp.tile(g, LANES // g.shape[0]).reshape(1, LANES)


def _layer(x, mod, l, w, rope_tabs, caches, state0, is_context):
    b, t, d = x.shape
    m = b * t
    rows_per_cond = m // mod.shape[0]
    x2d = x.reshape(m, d)
    shift, scale, gate = (mod[:, None, j * d:(j + 1) * d] for j in range(3))
    proj = _in_proj(x2d, shift, scale, w["norm_gain"][l].reshape(1, d), w["w_in"], l, rows_per_cond)
    gains = (_tile_gain(w["diff_q_norm"][l]), _tile_gain(w["diff_k_norm"][l]),
             _tile_gain(w["win_q_norm"][l]), _tile_gain(w["win_k_norm"][l]))
    prepped = _prep(proj, gains, rope_tabs, t, keep_f32=is_context)
    dq, dk, wq, wk, wv = (prepped[j].reshape(b, t, -1) for j in (0, 1, 3, 4, 5))
    dv = prepped[2]
    proj3 = proj.reshape(b, t, -1)
    lam_init = 0.8 - 0.6 * math.exp(-0.3 * l)
    if is_context:
        dctx = wctx = None
    else:
        ck_d, cv_d, ck_w, cv_w = caches
        dctx = (ck_d.reshape(b, -1, BRANCH_W).astype(BF16),
                jnp.transpose(cv_d, (0, 2, 3, 1)).astype(BF16))
        wctx = (ck_w.reshape(b, -1, WIN_KV_HEADS * LANES).astype(BF16),
                cv_w.reshape(b, -1, WIN_KV_HEADS * LANES).astype(BF16))
    yd = _diff_attn(dq, dk, dv, dctx, proj3, w["diff_lambda"][l], _tile_gain(w["diff_subln"][l]), lam_init)
    o_f, o_b, s_out = _hgrn(proj3, w["hg_lb"], state0, l)
    yw = _win_attn(wq, wk, wv, wctx, proj3, w["win_sink"][l], banded=not is_context)
    merged = _merge(yd.reshape(m, -1), o_f.reshape(m, -1), o_b.reshape(m, -1), yw.reshape(m, -1), proj,
                    _tile_gain(w["hg_out_norm"][l]), w["w_branch"], l)
    y = _out_proj(x2d, gate, merged, w["w_out"], l, rows_per_cond).reshape(b, t, d)
    if not is_context:
        return y, None
    new = (prepped[6].reshape(b, t, N_HEADS, 2, DIFF_QK_DIM),
           proj3[:, :, OFF_DV:OFF_DV + BRANCH_W].reshape(b, t, N_HEADS, LANES),
           prepped[7].reshape(b, t, WIN_KV_HEADS, LANES),
           proj3[:, :, OFF_WV:OFF_WV + WIN_KV_HEADS * LANES].reshape(b, t, WIN_KV_HEADS, LANES),
           s_out)
    return y, new


def kernel(x_prompt, x_sample, cache_diff_k, cache_diff_v, cache_win_k, cache_win_v, state_hgrn, c, c_ctx, norm_gain, w_ada, b_ada, w_in, diff_q_norm, diff_k_norm, diff_lambda, diff_subln, hg_lb, hg_out_norm, win_q_norm, win_k_norm, win_sink, w_branch, w_out):
    depth = w_in.shape[0]
    dec_b, dec_t, d = x_sample.shape
    w = {"norm_gain": norm_gain, "w_in": w_in.astype(BF16), "diff_q_norm": diff_q_norm,
         "diff_k_norm": diff_k_norm, "diff_lambda": diff_lambda, "diff_subln": diff_subln,
         "hg_lb": hg_lb, "hg_out_norm": hg_out_norm, "win_q_norm": win_q_norm, "win_k_norm": win_k_norm,
         "win_sink": win_sink, "w_branch": w_branch.astype(BF16), "w_out": w_out.astype(BF16)}

    cond_rows = 8 * ((1 + dec_b + 7) // 8)
    cond = jnp.zeros((cond_rows, d), F32).at[0].set(c_ctx).at[1:1 + dec_b].set(c)
    mod = _ada(cond, w_ada, b_ada)

    y_prompt = x_prompt
    news = []
    zero_state = jnp.zeros((x_prompt.shape[0],) + state_hgrn.shape[2:], F32)
    for l in range(depth):
        y_prompt, new = _layer(y_prompt, mod[l, 0:1], l, w, None, None, zero_state, True)
        news.append(new)

    ropes = _rope_tables(dec_t, DIFF_QK_DIM) + _rope_tables(dec_t, LANES)
    y_sample = x_sample
    for l in range(depth):
        caches = (cache_diff_k[:, l], cache_diff_v[:, l], cache_win_k[:, l], cache_win_v[:, l])
        y_sample, _ = _layer(y_sample, mod[l, 1:1 + dec_b], l, w, ropes, caches, state_hgrn[:, l], False)

    stacked = tuple(jnp.stack([n[j] for n in news], axis=1) for j in range(5))
    return (y_prompt, y_sample) + stacked
```

```python
import functools
import math

import numpy as np
import jax
import jax.numpy as jnp
from jax import lax
from jax.experimental import pallas as pl
from jax.experimental.pallas import tpu as pltpu

F32 = jnp.float32
BF16 = jnp.bfloat16

NORM_EPS = 1e-6
F_FLOOR = 1e-30
ROPE_BASE = 10000.0
GRID_W = 64
WINDOW = 128

LANES = 128
N_HEADS = 8
WIN_KV_HEADS = 4
WIN_GROUP = 2
DIFF_QK_DIM = 64
BRANCH_W = N_HEADS * LANES
HG_CHUNK = 64
HG_LEVELS = 6
HG_MXU_LEVELS = 3
DIFF_KV_CHUNK = 512
WIN_BLOCKS_PER_STEP = 8
DIFF_Q_TILE = 2048
LOG2E = math.log2(math.e)

OFF_DQ, OFF_DK, OFF_DV, OFF_DG = 0, 1024, 2048, 3072
OFF_HQ, OFF_HFF, OFF_HFB, OFF_HI, OFF_HG = 4096, 5120, 6144, 7168, 8192
OFF_WQ, OFF_WK, OFF_WV, OFF_WG = 9216, 10240, 10752, 11264
OFF_MG = 12288

VMEM_LIMIT = 48 * 1024 * 1024

NT_DIMS = (((1,), (1,)), ((), ()))
TN_DIMS = (((0,), (0,)), ((), ()))


def _params(*sem):
    return pltpu.CompilerParams(dimension_semantics=sem, vmem_limit_bytes=VMEM_LIMIT)


def _tile(n, preferred):
    t = min(n, preferred)
    while n % t:
        t //= 2
    return t


def _dot(a, b):
    return jnp.dot(a, b, preferred_element_type=F32)


def _dot_nt(a, b):
    return lax.dot_general(a, b, NT_DIMS, preferred_element_type=F32)


def _sigmoid(x):
    return 0.5 * jnp.tanh(0.5 * x) + 0.5


def _silu(x):
    return x * _sigmoid(x)


def _split_bf16(x, terms):
    out = []
    for _ in range(terms - 1):
        hi = x.astype(BF16)
        out.append(hi)
        x = x - hi.astype(F32)
    out.append(x.astype(BF16))
    return out


def _ada_kernel(c_ref, w_ref, b_ref, o_ref):
    a = _silu(c_ref[...])
    a_hi, a_lo = _split_bf16(a, 2)
    w_hi, w_lo = _split_bf16(w_ref[...], 2)
    acc = _dot(a_hi, w_hi) + _dot(a_lo, w_hi) + _dot(a_hi, w_lo)
    o_ref[...] = acc + b_ref[...]


def _ada(cond, w_ada, b_ada):
    depth, d, n = w_ada.shape
    tn = _tile(n, 512)
    rows = cond.shape[0]
    return pl.pallas_call(
        _ada_kernel,
        grid=(depth, n // tn),
        in_specs=[
            pl.BlockSpec((rows, d), lambda l, j: (0, 0)),
            pl.BlockSpec((None, d, tn), lambda l, j: (l, 0, j)),
            pl.BlockSpec((None, 1, tn), lambda l, j: (l, 0, j)),
        ],
        out_specs=pl.BlockSpec((None, rows, tn), lambda l, j: (l, 0, j)),
        out_shape=jax.ShapeDtypeStruct((depth, rows, n), F32),
        compiler_params=_params("parallel", "parallel"),
        name="ada_mod",
    )(cond, w_ada, b_ada.reshape(depth, 1, n))


def _in_proj_kernel(x_ref, shift_ref, scale_ref, gain_ref, w_ref, o_ref, h_ref):
    @pl.when(pl.program_id(1) == 0)
    def _():
        x = x_ref[...]
        ms = jnp.mean(x * x, axis=-1, keepdims=True)
        y = x * lax.rsqrt(ms + NORM_EPS) * gain_ref[...]
        h_ref[...] = (y * (1.0 + scale_ref[...]) + shift_ref[...]).astype(BF16)

    o_ref[...] = _dot(h_ref[...], w_ref[...])


def _in_proj(x2d, shift, scale, gain, w_in, layer, rows_per_cond):
    m, d = x2d.shape
    n = w_in.shape[-1]
    tm = _tile(rows_per_cond, 1024)
    tn = _tile(n, 1024)
    return pl.pallas_call(
        _in_proj_kernel,
        grid=(m // tm, n // tn),
        in_specs=[
            pl.BlockSpec((tm, d), lambda i, j: (i, 0)),
            pl.BlockSpec((None, 1, d), lambda i, j: (i * tm // rows_per_cond, 0, 0)),
            pl.BlockSpec((None, 1, d), lambda i, j: (i * tm // rows_per_cond, 0, 0)),
            pl.BlockSpec((1, d), lambda i, j: (0, 0)),
            pl.BlockSpec((None, d, tn), lambda i, j: (layer, 0, j)),
        ],
        out_specs=pl.BlockSpec((tm, tn), lambda i, j: (i, j)),
        out_shape=jax.ShapeDtypeStruct((m, n), F32),
        scratch_shapes=[pltpu.VMEM((tm, d), BF16)],
        compiler_params=_params("parallel", "arbitrary"),
        name="in_proj",
    )(x2d, shift, scale, gain, w_in)


def _head_norm(x, gain, group):
    x2 = x * x
    if group == LANES:
        ms = jnp.sum(x2, axis=-1, keepdims=True) * (1.0 / LANES)
    else:
        lo = lax.broadcasted_iota(jnp.int32, x.shape, 1) < group
        s_lo = jnp.sum(jnp.where(lo, x2, 0.0), axis=-1, keepdims=True)
        s_hi = jnp.sum(jnp.where(lo, 0.0, x2), axis=-1, keepdims=True)
        ms = jnp.where(lo, s_lo, s_hi) * (1.0 / group)
    return x * lax.rsqrt(ms + NORM_EPS) * gain


PREP_W = 2 * LANES


def _group_mean_matrix(group):
    i = np.arange(PREP_W)
    return ((i[:, None] // group) == (i[None, :] // group)).astype(np.float32) / group


def _rotate_half_matrix(d):
    quarter = d // 4
    i = np.arange(PREP_W)
    first = (i % (2 * quarter)) < quarter
    mat = np.zeros((PREP_W, PREP_W), np.float32)
    mat[i[first] + quarter, i[first]] = -1.0
    mat[i[~first] - quarter, i[~first]] = 1.0
    return mat


def _prep_kernel(*refs, rope, keep_f32):
    it = iter(refs)
    dq_ref, dk_ref, dv_ref, wq_ref, wk_ref, wv_ref = (next(it) for _ in range(6))
    gdq_ref, gdk_ref, gwq_ref, gwk_ref, md_ref, mw_ref = (next(it) for _ in range(6))
    if rope:
        rd_ref, rw_ref, cd_ref, sd_ref, cw_ref, sw_ref = (next(it) for _ in range(6))
    if keep_f32:
        for _ in range(4):
            next(it)
    odq, odk, odv, owq, owk, owv = (next(it) for _ in range(6))
    if keep_f32:
        odk32, odv32, owk32, owv32 = (next(it) for _ in range(4))

    def pair(src, j, gain_ref, mean_ref, rot_ref, cos_ref, sin_ref):
        x = src[:, j * PREP_W:(j + 1) * PREP_W]
        hi, lo = _split_bf16(x * x, 2)
        ms = _dot(hi, mean_ref[...]) + _dot(lo, mean_ref[...])
        y = x * lax.rsqrt(ms + NORM_EPS) * gain_ref[...]
        if rope:
            y = y * cos_ref[...] + _dot(y.astype(BF16), rot_ref[...]) * sin_ref[...]
        return y

    dtabs = (rd_ref, cd_ref, sd_ref) if rope else (None, None, None)
    wtabs = (rw_ref, cw_ref, sw_ref) if rope else (None, None, None)
    for j in range(N_HEADS // 2):
        sl = slice(j * PREP_W, (j + 1) * PREP_W)
        q = pair(dq_ref, j, gdq_ref, md_ref, *dtabs)
        odq[:, sl] = (q * (DIFF_QK_DIM ** -0.5 * LOG2E)).astype(BF16)
        k = pair(dk_ref, j, gdk_ref, md_ref, *dtabs)
        odk[:, sl] = k.astype(BF16)
        if keep_f32:
            odk32[:, sl] = k
        q = pair(wq_ref, j, gwq_ref, mw_ref, *wtabs)
        owq[:, sl] = (q * (LANES ** -0.5 * LOG2E)).astype(BF16)
    for j in range(WIN_KV_HEADS // 2):
        sl = slice(j * PREP_W, (j + 1) * PREP_W)
        k = pair(wk_ref, j, gwk_ref, mw_ref, *wtabs)
        owk[:, sl] = k.astype(BF16)
        if keep_f32:
            owk32[:, sl] = k
    for j in range(N_HEADS):
        odv[j] = dv_ref[:, j * LANES:(j + 1) * LANES].T.astype(BF16)
    owv[...] = wv_ref[...].astype(BF16)
    if keep_f32:
        odv32[...] = dv_ref[...]
        owv32[...] = wv_ref[...]


def _prep(proj, gains, rope_tabs, seq_len, caches=None, layer=0):
    m = proj.shape[0]
    keep_f32 = caches is not None
    tr = _tile(seq_len, DIFF_KV_CHUNK)
    nt = seq_len // tr
    w8, w4 = BRANCH_W, WIN_KV_HEADS * LANES
    rope = rope_tabs is not None

    def col(width, off):
        return pl.BlockSpec((tr, width), lambda i: (i, off // width))

    small = pl.BlockSpec((1, PREP_W), lambda i: (0, 0))
    square = pl.BlockSpec((PREP_W, PREP_W), lambda i: (0, 0))
    in_specs = [col(w8, OFF_DQ), col(w8, OFF_DK), col(w8, OFF_DV), col(w8, OFF_WQ),
                col(w4, OFF_WK), col(w4, OFF_WV), small, small, small, small, square, square]
    args = [proj] * 6 + list(gains) + [jnp.asarray(_group_mean_matrix(DIFF_QK_DIM), BF16),
                                        jnp.asarray(_group_mean_matrix(LANES), BF16)]
    if rope:
        tab = pl.BlockSpec((tr, PREP_W), lambda i: (i % nt, 0))
        in_specs += [square, square] + [tab] * 4
        args += [jnp.asarray(_rotate_half_matrix(DIFF_QK_DIM), BF16),
                 jnp.asarray(_rotate_half_matrix(LANES), BF16)] + list(rope_tabs)
    out8 = pl.BlockSpec((tr, w8), lambda i: (i, 0))
    out4 = pl.BlockSpec((tr, w4), lambda i: (i, 0))
    out_vt = pl.BlockSpec((None, N_HEADS, None, LANES, tr), lambda i: (i // nt, 0, i % nt, 0, 0))
    out_specs = [out8, out8, out_vt, out8, out4, out4]
    out_shape = [jax.ShapeDtypeStruct((m, w8), BF16), jax.ShapeDtypeStruct((m, w8), BF16),
                 jax.ShapeDtypeStruct((m // seq_len, N_HEADS, nt, LANES, tr), BF16),
                 jax.ShapeDtypeStruct((m, w8), BF16), jax.ShapeDtypeStruct((m, w4), BF16),
                 jax.ShapeDtypeStruct((m, w4), BF16)]
    aliases = {}
    if keep_f32:
        for j, cache in enumerate(caches):
            aliases[len(args)] = len(out_shape)
            in_specs.append(pl.BlockSpec(memory_space=pl.ANY))
            args.append(cache)
            out_specs.append(pl.BlockSpec((None, None, tr, cache.shape[-1]),
                                          lambda i: (i // nt, layer, i % nt, 0)))
            out_shape.append(jax.ShapeDtypeStruct(cache.shape, F32))
    return pl.pallas_call(
        functools.partial(_prep_kernel, rope=rope, keep_f32=keep_f32),
        grid=(m // tr,),
        in_specs=in_specs,
        out_specs=out_specs,
        out_shape=out_shape,
        input_output_aliases=aliases,
        compiler_params=_params("parallel"),
        name="qk_prep",
    )(*args)


def _diff_attn_kernel(*refs, lam_init, n_chunks, ck, has_ctx):
    it = iter(refs)
    lam_ref, subln_ref, q_ref, k_ref, vt_ref = (next(it) for _ in range(5))
    if has_ctx:
        kc_ref, vct_ref = next(it), next(it)
    g_ref, o_ref, acc_ref = next(it), next(it), next(it)

    tq = q_ref.shape[0]
    q = q_ref[...]
    first = lax.broadcasted_iota(jnp.int32, q.shape, 1) < DIFF_QK_DIM
    zero = jnp.zeros_like(q)
    q2 = jnp.concatenate([jnp.where(first, q, zero), jnp.where(first, zero, q)], axis=0)

    def step(kb, vtb, carry, is_first):
        m, l = carry
        s = _dot_nt(kb, q2)
        m_new = jnp.maximum(m, jnp.max(s, axis=0, keepdims=True))
        p = jnp.exp2(s - m_new)
        pv = _dot(vtb, p.astype(BF16))
        if is_first:
            l = jnp.sum(p, axis=0, keepdims=True)
            acc_ref[...] = pv
        else:
            alpha = jnp.exp2(m - m_new)
            l = alpha * l + jnp.sum(p, axis=0, keepdims=True)
            acc_ref[...] = alpha * acc_ref[...] + pv
        return m_new, l

    none = jnp.zeros((1, 2 * tq), F32)
    carry = step(k_ref[pl.ds(0, ck), :], vt_ref[0], (none, none), True)

    def body(c, carry):
        r = pl.multiple_of(c * ck, ck)
        return step(k_ref[pl.ds(r, ck), :], vt_ref[c], carry, False)

    carry = lax.fori_loop(1, n_chunks, body, carry)
    if has_ctx:
        carry = step(kc_ref[...], vct_ref[...], carry, False)

    lv = lam_ref[...]
    lam = (jnp.exp(jnp.sum(lv[0:1] * lv[1:2], axis=-1, keepdims=True))
           - jnp.exp(jnp.sum(lv[2:3] * lv[3:4], axis=-1, keepdims=True)) + lam_init)
    r = acc_ref[...] / carry[1]
    o = (r[:, :tq] - lam * r[:, tq:]).T
    ms = jnp.mean(o * o, axis=-1, keepdims=True)
    y = o * lax.rsqrt(ms + NORM_EPS) * subln_ref[...]
    y = y * (1.0 - lam_init)
    o_ref[...] = (y * _silu(g_ref[...])).astype(BF16)


def _diff_attn(q, k, vt, ctx, proj3, lam_vecs, subln, lam_init):
    b, t, _ = q.shape
    n_chunks, ck = vt.shape[2], vt.shape[4]
    tq = _tile(t, DIFF_Q_TILE)
    has_ctx = ctx is not None
    qspec = pl.BlockSpec((None, tq, LANES), lambda bi, h, i: (bi, i, h))
    in_specs = [pl.BlockSpec((4, DIFF_QK_DIM), lambda bi, h, i: (0, 0)),
                pl.BlockSpec((1, LANES), lambda bi, h, i: (0, 0)),
                qspec,
                pl.BlockSpec((None, t, LANES), lambda bi, h, i: (bi, 0, h)),
                pl.BlockSpec((None, None, n_chunks, LANES, ck), lambda bi, h, i: (bi, h, 0, 0, 0))]
    args = [lam_vecs, subln, q, k, vt]
    if has_ctx:
        p = ctx[0].shape[1]
        in_specs += [pl.BlockSpec((None, p, LANES), lambda bi, h, i: (bi, 0, h)),
                     pl.BlockSpec((None, None, LANES, p), lambda bi, h, i: (bi, h, 0, 0))]
        args += list(ctx)
    in_specs.append(pl.BlockSpec((None, tq, LANES), lambda bi, h, i: (bi, i, OFF_DG // LANES + h)))
    args.append(proj3)
    return pl.pallas_call(
        functools.partial(_diff_attn_kernel, lam_init=lam_init, n_chunks=n_chunks, ck=ck, has_ctx=has_ctx),
        grid=(b, N_HEADS, t // tq),
        in_specs=in_specs,
        out_specs=qspec,
        out_shape=jax.ShapeDtypeStruct((b, t, BRANCH_W), BF16),
        scratch_shapes=[pltpu.VMEM((LANES, 2 * tq), F32)],
        compiler_params=_params("parallel", "parallel", "parallel"),
        name="diff_attn",
    )(*args)


def _win_attn_kernel(*refs, banded, n_blocks, blocks_per_step, has_ctx):
    it = iter(refs)
    sink_ref, q_ref, k_ref, v_ref = (next(it) for _ in range(4))
    if has_ctx:
        kc_ref, vc_ref = next(it), next(it)
    g_ref, o_ref = next(it), next(it)

    kvh = pl.program_id(1)
    w = WINDOW
    gw = WIN_GROUP * w
    sink = jnp.concatenate([jnp.full((w, 1), sink_ref[kvh * WIN_GROUP + j] * LOG2E, F32)
                            for j in range(WIN_GROUP)], axis=0)
    if banded:
        row = lax.broadcasted_iota(jnp.int32, (gw, w), 0) & (w - 1)
        col = lax.broadcasted_iota(jnp.int32, (gw, w), 1)
    shared = []
    if not banded:
        shared.append((k_ref[...], v_ref[...], None))
    if has_ctx:
        shared.append((kc_ref[...], vc_ref[...], None))

    for u in range(blocks_per_step):
        rows = slice(u * w, (u + 1) * w)
        q2 = jnp.concatenate([q_ref[rows, j * LANES:(j + 1) * LANES] for j in range(WIN_GROUP)], axis=0)
        segs = []
        if banded:
            n = pl.program_id(2) * blocks_per_step + u
            r_prev = pl.multiple_of(jnp.maximum(n - 1, 0) * w, w)
            r_cur = pl.multiple_of(n * w, w)
            r_next = pl.multiple_of(jnp.minimum(n + 1, n_blocks - 1) * w, w)
            keep_prev = (col >= row) & (n > 0)
            keep_next = (col <= row) & (n < n_blocks - 1)
            segs = [(k_ref[pl.ds(r_prev, w), :], v_ref[pl.ds(r_prev, w), :], keep_prev),
                    (k_ref[pl.ds(r_cur, w), :], v_ref[pl.ds(r_cur, w), :], None),
                    (k_ref[pl.ds(r_next, w), :], v_ref[pl.ds(r_next, w), :], keep_next)]
        segs += shared
        scores = []
        for kb, _, keep in segs:
            s = _dot_nt(q2, kb)
            scores.append(s if keep is None else jnp.where(keep, s, -jnp.inf))
        s = jnp.concatenate(scores, axis=1)
        m = jnp.maximum(sink, jnp.max(s, axis=-1, keepdims=True))
        p = jnp.exp2(s - m)
        l = jnp.exp2(sink - m) + jnp.sum(p, axis=-1, keepdims=True)
        v_all = jnp.concatenate([vb for _, vb, _ in segs], axis=0)
        o = _dot(p.astype(BF16), v_all) / l
        for j in range(WIN_GROUP):
            sl = slice(j * LANES, (j + 1) * LANES)
            o_ref[rows, sl] = (o[j * w:(j + 1) * w] * _silu(g_ref[rows, sl])).astype(BF16)


def _win_attn(q, k, v, ctx, proj3, sink, banded):
    b, t, _ = q.shape
    nb = t // WINDOW
    gw = WIN_GROUP * LANES
    has_ctx = ctx is not None
    per_step = _tile(nb, WIN_BLOCKS_PER_STEP)
    tq = per_step * WINDOW
    qspec = pl.BlockSpec((None, tq, gw), lambda bi, h, i: (bi, i, h))
    kvspec = pl.BlockSpec((None, t, LANES), lambda bi, h, i: (bi, 0, h))
    in_specs = [pl.BlockSpec(memory_space=pltpu.SMEM), qspec, kvspec, kvspec]
    args = [sink, q, k, v]
    if has_ctx:
        p = ctx[0].shape[1]
        cspec = pl.BlockSpec((None, p, LANES), lambda bi, h, i: (bi, 0, h))
        in_specs += [cspec, cspec]
        args += list(ctx)
    in_specs.append(pl.BlockSpec((None, tq, gw), lambda bi, h, i: (bi, i, OFF_WG // gw + h)))
    args.append(proj3)
    return pl.pallas_call(
        functools.partial(_win_attn_kernel, banded=banded, n_blocks=nb, blocks_per_step=per_step,
                          has_ctx=has_ctx),
        grid=(b, WIN_KV_HEADS, nb // per_step),
        in_specs=in_specs,
        out_specs=qspec,
        out_shape=jax.ShapeDtypeStruct((b, t, BRANCH_W), BF16),
        compiler_params=_params("parallel", "parallel", "parallel"),
        name="win_attn",
    )(*args)


def _hgrn_tables(reverse):
    c = HG_CHUNK
    t = np.arange(c)[:, None]
    u = np.arange(c)[None, :]
    mats, masks = [], []
    for m in range(HG_LEVELS):
        half, blk = 1 << m, 2 << m
        r = t - t % blk + half - 1
        right = (t % blk) >= half
        if not reverse:
            e_q = (u >= r + 1) & (u <= t)
            e_k = (u >= t + 1) & (u <= r)
            mats.append(np.where(right, e_q, e_k))
            masks.append(right & ~right.T & (t // blk == u // blk))
        else:
            e_q = (u >= t) & (u <= r)
            e_k = (u >= r + 1) & (u <= t - 1)
            mats.append(np.where(right, e_k, e_q))
            masks.append(~right & right.T & (t // blk == u // blk))
    mats = mats[:HG_MXU_LEVELS] + [(u <= t) if not reverse else (u >= t)]
    masks.append(t == u)
    return (np.concatenate(mats, 0).astype(np.float32), np.stack(masks).astype(np.float32))


def _hgrn_constants():
    c = HG_CHUNK
    tabs = [_hgrn_tables(False), _hgrn_tables(True)]
    mats = np.stack([np.tile(tb[0], (1, 3)) for tb in tabs])
    masks = np.zeros((HG_LEVELS + 1, 4, c, LANES), np.float32)
    for a in range(4):
        lo = (a % 2) * c
        masks[:, a, :, lo:lo + c] = tabs[a // 2][1]
    return mats, masks


def _hgrn_kernel(lb_ref, s0_ref, mat_ref, mask_ref,
                 qf_ref, vf_ref, zf_ref, qb_ref, vb_ref, zb_ref, *rest, layer, n_chunks):
    of_ref, ob_ref, sout_ref, st_ref = rest[-4:]
    i = pl.program_id(2)
    c = HG_CHUNK
    pair = 2
    pw = pair * LANES

    @pl.when(i == 0)
    def _():
        for d in range(2):
            for hh in range(pair):
                st_ref[2 * d + hh] = s0_ref[d, hh]

    def lower_bound(d):
        rows = [lb_ref[d, r:r + 1, :] for r in range(lb_ref.shape[1])]
        mx = functools.reduce(jnp.maximum, rows)
        es = [jnp.exp(r - mx) for r in rows]
        den = functools.reduce(lambda a, b_: a + b_, es)
        ps = [e / den for e in es]
        return functools.reduce(lambda a, b_: a + b_, ps[:layer + 1]) - ps[0]

    bit_row = lax.broadcasted_iota(jnp.int32, (c, pw), 0)

    def side_rows(m, d, on_query_rows, on_key_rows):
        half = 1 << m
        if half % 8:
            return jnp.where(((bit_row >> m) & 1) == (1 - d), on_query_rows, on_key_rows)
        slabs = [(on_query_rows if (r0 // half) % 2 == 1 - d else on_key_rows)[r0:r0 + half]
                 for r0 in range(0, c, half)]
        return jnp.concatenate(slabs, axis=0)

    def decays(d, z, lb):
        e = jnp.exp(-jnp.abs(z))
        r = 1.0 / (1.0 + e)
        er = e * r
        pos = z >= 0.0
        f = lb + (1.0 - lb) * jnp.where(pos, r, er)
        kk = (1.0 - lb) * jnp.where(pos, er, r)
        logf = jnp.log(jnp.maximum(f, F_FLOOR)) * LOG2E
        parts = jnp.concatenate(_split_bf16(logf, 3), axis=0)
        e_small = _dot(mat_ref[d], parts)
        run = e_small[HG_MXU_LEVELS * c:]
        levels = [jnp.exp2(e_small[m * c:(m + 1) * c]) for m in range(HG_MXU_LEVELS)]
        for m in range(HG_MXU_LEVELS, HG_LEVELS):
            half, blk = 1 << m, 2 << m
            edge = half - 1 + d
            rho = jnp.concatenate(
                [jnp.broadcast_to(run[b0 + edge:b0 + edge + 1], (blk, pw)) for b0 in range(0, c, blk)], axis=0)
            diff = run - rho
            levels.append(jnp.exp2(side_rows(m, d, diff, -diff)))
        last = (c - 1) if d == 0 else 0
        total = run[last:last + 1]
        return kk, levels, jnp.exp2(run), jnp.exp2(total - run), jnp.exp2(total)

    lbs = (lower_bound(0), lower_bound(1))

    def body(ci, _):
        rows = (pl.ds(pl.multiple_of(ci * c, c), c), pl.ds(pl.multiple_of((n_chunks - 1 - ci) * c, c), c))
        qs = (qf_ref[rows[0], :], qb_ref[rows[1], :])
        vs = (vf_ref[rows[0], :].astype(BF16), vb_ref[rows[1], :].astype(BF16))
        zs = (zf_ref[rows[0], :], zb_ref[rows[1], :])
        dec = [decays(d, zs[d], lbs[d]) for d in range(2)]

        def stacked(fn):
            wide = [fn(d).astype(BF16) for d in range(2)]
            return jnp.concatenate([wide[a // 2][:, (a % 2) * LANES:(a % 2 + 1) * LANES] for a in range(4)],
                                   axis=0)

        def pair_weights(g, m):
            return [mask_ref[m, a] * g[a * c:(a + 1) * c, (a // 2) * LANES:(a // 2 + 1) * LANES]
                    for a in range(4)]

        acc = pair_weights(_dot_nt(stacked(lambda d: qs[d]), stacked(lambda d: dec[d][0])), HG_LEVELS)
        for m in range(HG_LEVELS):
            def level_rows(d, m=m):
                return side_rows(m, d, qs[d], dec[d][0]) * dec[d][1][m]
            lm = stacked(level_rows)
            acc = [x + y for x, y in zip(acc, pair_weights(_dot_nt(lm, lm), m))]

        outs = (of_ref, ob_ref)
        for a in range(4):
            d, hh = a // 2, a % 2
            sl = slice(hh * LANES, (hh + 1) * LANES)
            kk, _, from_start, to_end, whole = dec[d]
            v = vs[d][:, sl]
            qe = (qs[d][:, sl] * from_start[:, sl]).astype(BF16)
            ke = (kk[:, sl] * to_end[:, sl]).astype(BF16)
            st = st_ref[a]
            lhs = jnp.concatenate([acc[a].astype(BF16), qe], axis=1)
            rhs = jnp.concatenate([v, v, st.astype(BF16)], axis=0)
            outs[d][rows[d], sl] = _dot(lhs, rhs)
            decay_col = jnp.broadcast_to(whole[:, sl], (LANES, LANES)).T
            st_ref[a] = st * decay_col + lax.dot_general(ke, v, TN_DIMS, preferred_element_type=F32)
        return 0

    lax.fori_loop(0, n_chunks, body, 0, unroll=4)

    @pl.when(i == pl.num_programs(2) - 1)
    def _():
        for d in range(2):
            for hh in range(pair):
                sout_ref[d, hh] = st_ref[2 * d + hh]


def _hgrn(proj3, hg_lb, state0, layer, states_out=None):
    b, t, _ = proj3.shape
    tb = _tile(t, 512)
    nt = t // tb
    pw = 2 * LANES
    mats, masks = _hgrn_constants()
    mats = jnp.asarray(mats, BF16)
    masks = jnp.asarray(masks, F32)

    def fwd(off):
        return pl.BlockSpec((None, tb, pw), lambda bi, hp, i: (bi, i, off // pw + hp))

    def bwd(off):
        return pl.BlockSpec((None, tb, pw), lambda bi, hp, i: (bi, nt - 1 - i, off // pw + hp))

    st_spec = pl.BlockSpec((None, 2, 2, LANES, LANES), lambda bi, hp, i: (bi, 0, hp, 0, 0))
    in_specs = [
        pl.BlockSpec((2, hg_lb.shape[1], pw), lambda bi, hp, i: (0, 0, hp)),
        st_spec,
        pl.BlockSpec(mats.shape, lambda bi, hp, i: (0, 0, 0)),
        pl.BlockSpec(masks.shape, lambda bi, hp, i: (0, 0, 0, 0)),
        fwd(OFF_HQ), fwd(OFF_HI), fwd(OFF_HFF),
        bwd(OFF_HQ), bwd(OFF_HI), bwd(OFF_HFB),
    ]
    args = [hg_lb, state0, mats, masks, proj3, proj3, proj3, proj3, proj3, proj3]
    aliases = {}
    if states_out is None:
        st_out_spec, st_out_shape = st_spec, jax.ShapeDtypeStruct(state0.shape, F32)
    else:
        aliases[len(args)] = 2
        in_specs.append(pl.BlockSpec(memory_space=pl.ANY))
        args.append(states_out)
        st_out_spec = pl.BlockSpec((None, None, 2, 2, LANES, LANES), lambda bi, hp, i: (bi, layer, 0, hp, 0, 0))
        st_out_shape = jax.ShapeDtypeStruct(states_out.shape, F32)
    return pl.pallas_call(
        functools.partial(_hgrn_kernel, layer=layer, n_chunks=tb // HG_CHUNK),
        grid=(b, N_HEADS // 2, nt),
        in_specs=in_specs,
        out_specs=[
            pl.BlockSpec((None, tb, pw), lambda bi, hp, i: (bi, i, hp)),
            pl.BlockSpec((None, tb, pw), lambda bi, hp, i: (bi, nt - 1 - i, hp)),
            st_out_spec,
        ],
        out_shape=[
            jax.ShapeDtypeStruct((b, t, BRANCH_W), F32),
            jax.ShapeDtypeStruct((b, t, BRANCH_W), F32),
            st_out_shape,
        ],
        scratch_shapes=[pltpu.VMEM((4, LANES, LANES), F32)],
        input_output_aliases=aliases,
        compiler_params=_params("parallel", "parallel", "arbitrary"),
        name="hgrn_scan",
    )(*args)


def _merge_kernel(yd_ref, of_ref, ob_ref, yw_ref, hg_ref, gain_ref, mg0_ref, mg1_ref, mg2_ref, w_ref,
                  o_ref, yh_ref):
    @pl.when(pl.program_id(1) == 0)
    def _():
        for j in range(N_HEADS):
            sl = slice(j * LANES, (j + 1) * LANES)
            y = _head_norm(of_ref[:, sl] + ob_ref[:, sl], gain_ref[...], LANES)
            yh_ref[:, sl] = (y * _silu(hg_ref[:, sl])).astype(BF16)

    acc = _sigmoid(mg0_ref[...]) * _dot(yd_ref[...], w_ref[0])
    acc = acc + _sigmoid(mg1_ref[...]) * _dot(yh_ref[...], w_ref[1])
    acc = acc + _sigmoid(mg2_ref[...]) * _dot(yw_ref[...], w_ref[2])
    o_ref[...] = acc.astype(BF16)


def _merge(yd, o_f, o_b, yw, proj, hg_gain, w_branch, layer):
    m = yd.shape[0]
    d = w_branch.shape[-1]
    tm = _tile(m, 1024)
    tn = _tile(d, 256)
    row = lambda width: pl.BlockSpec((tm, width), lambda i, j: (i, 0))

    def mg(nb):
        return pl.BlockSpec((tm, tn), lambda i, j: (i, (OFF_MG + nb * d) // tn + j))

    return pl.pallas_call(
        _merge_kernel,
        grid=(m // tm, d // tn),
        in_specs=[row(BRANCH_W), row(BRANCH_W), row(BRANCH_W), row(BRANCH_W),
                  pl.BlockSpec((tm, BRANCH_W), lambda i, j: (i, OFF_HG // BRANCH_W)),
                  pl.BlockSpec((1, LANES), lambda i, j: (0, 0)),
                  mg(0), mg(1), mg(2),
                  pl.BlockSpec((None, 3, BRANCH_W, tn), lambda i, j: (layer, 0, 0, j))],
        out_specs=pl.BlockSpec((tm, tn), lambda i, j: (i, j)),
        out_shape=jax.ShapeDtypeStruct((m, d), BF16),
        scratch_shapes=[pltpu.VMEM((tm, BRANCH_W), BF16)],
        compiler_params=_params("parallel", "arbitrary"),
        name="branch_merge",
    )(yd, o_f, o_b, yw, proj, hg_gain, proj, proj, proj, w_branch)


def _out_proj_kernel(x_ref, gate_ref, m_ref, w_ref, o_ref):
    o_ref[...] = x_ref[...] + gate_ref[...] * _dot(m_ref[...], w_ref[...])


def _out_proj(x2d, gate, merged, w_out, layer, rows_per_cond):
    m, d = x2d.shape
    tm = _tile(rows_per_cond, 1024)
    tn = _tile(d, 512)
    return pl.pallas_call(
        _out_proj_kernel,
        grid=(m // tm, d // tn),
        in_specs=[
            pl.BlockSpec((tm, tn), lambda i, j: (i, j)),
            pl.BlockSpec((None, 1, tn), lambda i, j: (i * tm // rows_per_cond, 0, j)),
            pl.BlockSpec((tm, d), lambda i, j: (i, 0)),
            pl.BlockSpec((None, d, tn), lambda i, j: (layer, 0, j)),
        ],
        out_specs=pl.BlockSpec((tm, tn), lambda i, j: (i, j)),
        out_shape=jax.ShapeDtypeStruct((m, d), F32),
        compiler_params=_params("parallel", "parallel"),
        name="out_proj",
    )(x2d, gate, merged, w_out)


def _rope_tables(t, d):
    rows = t // GRID_W
    row = jnp.repeat(jnp.arange(rows, dtype=F32), GRID_W)
    col = jnp.tile(jnp.arange(GRID_W, dtype=F32), rows)
    quarter = d // 4
    inv = ROPE_BASE ** (-jnp.arange(quarter, dtype=F32) / quarter)
    ar = row[:, None] * inv
    ac = col[:, None] * inv
    ang = jnp.concatenate([ar, ar, ac, ac], axis=-1)
    return tuple(jnp.tile(a, (1, PREP_W // d)) for a in (jnp.cos(ang), jnp.sin(ang)))


def _tile_gain(g, width=LANES):
    return jnp.tile(g, width // g.shape[0]).reshape(1, width)


def _layer(x, mod, l, w, rope_tabs, caches, state0, new_caches):
    is_context = new_caches is not None
    b, t, d = x.shape
    m = b * t
    rows_per_cond = m // mod.shape[0]
    x2d = x.reshape(m, d)
    shift, scale, gate = (mod[:, None, j * d:(j + 1) * d] for j in range(3))
    proj = _in_proj(x2d, shift, scale, w["norm_gain"][l].reshape(1, d), w["w_in"], l, rows_per_cond)
    gains = tuple(_tile_gain(w[name][l], PREP_W)
                  for name in ("diff_q_norm", "diff_k_norm", "win_q_norm", "win_k_norm"))
    prepped = _prep(proj, gains, rope_tabs, t, new_caches[:4] if is_context else None, l)
    dq, dk, wq, wk, wv = (prepped[j].reshape(b, t, -1) for j in (0, 1, 3, 4, 5))
    dv = prepped[2]
    proj3 = proj.reshape(b, t, -1)
    lam_init = 0.8 - 0.6 * math.exp(-0.3 * l)
    if is_context:
        dctx = wctx = None
    else:
        ck_d, cv_d, ck_w, cv_w = caches
        dctx = (ck_d.reshape(b, -1, BRANCH_W).astype(BF16),
                jnp.transpose(cv_d, (0, 2, 3, 1)).astype(BF16))
        wctx = (ck_w.reshape(b, -1, WIN_KV_HEADS * LANES).astype(BF16),
                cv_w.reshape(b, -1, WIN_KV_HEADS * LANES).astype(BF16))
    yd = _diff_attn(dq, dk, dv, dctx, proj3, w["diff_lambda"][l], _tile_gain(w["diff_subln"][l]), lam_init)
    o_f, o_b, s_out = _hgrn(proj3, w["hg_lb"], state0, l, new_caches[4] if is_context else None)
    yw = _win_attn(wq, wk, wv, wctx, proj3, w["win_sink"][l], banded=not is_context)
    merged = _merge(yd.reshape(m, -1), o_f.reshape(m, -1), o_b.reshape(m, -1), yw.reshape(m, -1), proj,
                    _tile_gain(w["hg_out_norm"][l]), w["w_branch"], l)
    y = _out_proj(x2d, gate, merged, w["w_out"], l, rows_per_cond).reshape(b, t, d)
    if not is_context:
        return y, None
    return y, tuple(prepped[6:10]) + (s_out,)


def kernel(x_prompt, x_sample, cache_diff_k, cache_diff_v, cache_win_k, cache_win_v, state_hgrn, c, c_ctx, norm_gain, w_ada, b_ada, w_in, diff_q_norm, diff_k_norm, diff_lambda, diff_subln, hg_lb, hg_out_norm, win_q_norm, win_k_norm, win_sink, w_branch, w_out):
    depth = w_in.shape[0]
    dec_b, dec_t, d = x_sample.shape
    w = {"norm_gain": norm_gain, "w_in": w_in.astype(BF16), "diff_q_norm": diff_q_norm,
         "diff_k_norm": diff_k_norm, "diff_lambda": diff_lambda, "diff_subln": diff_subln,
         "hg_lb": hg_lb, "hg_out_norm": hg_out_norm, "win_q_norm": win_q_norm, "win_k_norm": win_k_norm,
         "win_sink": win_sink, "w_branch": w_branch.astype(BF16), "w_out": w_out.astype(BF16)}

    cond_rows = 8 * ((1 + dec_b + 7) // 8)
    cond = jnp.zeros((cond_rows, d), F32).at[0].set(c_ctx).at[1:1 + dec_b].set(c)
    mod = _ada(cond, w_ada, b_ada)

    y_prompt = x_prompt
    pb, pt = x_prompt.shape[:2]
    zero_state = jnp.zeros((pb,) + state_hgrn.shape[2:], F32)
    new = tuple(jnp.zeros((pb, depth, pt, width), F32)
                for width in (BRANCH_W, BRANCH_W, WIN_KV_HEADS * LANES, WIN_KV_HEADS * LANES))
    new += (jnp.zeros((pb, depth) + state_hgrn.shape[2:], F32),)
    for l in range(depth):
        y_prompt, new = _layer(y_prompt, mod[l, 0:1], l, w, None, None, zero_state, new)

    ropes = _rope_tables(dec_t, DIFF_QK_DIM) + _rope_tables(dec_t, LANES)
    y_sample = x_sample
    for l in range(depth):
        caches = (cache_diff_k[:, l], cache_diff_v[:, l], cache_win_k[:, l], cache_win_v[:, l])
        y_sample, _ = _layer(y_sample, mod[l, 1:1 + dec_b], l, w, ropes, caches, state_hgrn[:, l], None)

    return (y_prompt, y_sample,
            new[0].reshape(pb, depth, pt, N_HEADS, 2, DIFF_QK_DIM),
            new[1].reshape(pb, depth, pt, N_HEADS, LANES),
            new[2].reshape(pb, depth, pt, WIN_KV_HEADS, LANES),
            new[3].reshape(pb, depth, pt, WIN_KV_HEADS, LANES),
            new[4])
```

```python
import functools
import math

import numpy as np
import jax
import jax.numpy as jnp
from jax import lax
from jax.experimental import pallas as pl
from jax.experimental.pallas import tpu as pltpu

F32 = jnp.float32
BF16 = jnp.bfloat16

NORM_EPS = 1e-6
F_FLOOR = 1e-30
ROPE_BASE = 10000.0
GRID_W = 64
WINDOW = 128

LANES = 128
N_HEADS = 8
WIN_KV_HEADS = 4
WIN_GROUP = 2
DIFF_QK_DIM = 64
BRANCH_W = N_HEADS * LANES
HG_CHUNK = 64
HG_LEVELS = 6
HG_MXU_LEVELS = 3
DIFF_KV_CHUNK = 512
WIN_BLOCKS_PER_STEP = 8
DIFF_Q_TILE = 2048
LOG2E = math.log2(math.e)

OFF_DQ, OFF_DK, OFF_DV, OFF_DG = 0, 1024, 2048, 3072
OFF_HQ, OFF_HFF, OFF_HFB, OFF_HI, OFF_HG = 4096, 5120, 6144, 7168, 8192
OFF_WQ, OFF_WK, OFF_WV, OFF_WG = 9216, 10240, 10752, 11264
OFF_MG = 12288

VMEM_LIMIT = 48 * 1024 * 1024

NT_DIMS = (((1,), (1,)), ((), ()))
TN_DIMS = (((0,), (0,)), ((), ()))


def _params(*sem):
    return pltpu.CompilerParams(dimension_semantics=sem, vmem_limit_bytes=VMEM_LIMIT)


def _tile(n, preferred):
    t = min(n, preferred)
    while n % t:
        t //= 2
    return t


def _dot(a, b):
    return jnp.dot(a, b, preferred_element_type=F32)


def _dot_nt(a, b):
    return lax.dot_general(a, b, NT_DIMS, preferred_element_type=F32)


def _sigmoid(x):
    return 0.5 * jnp.tanh(0.5 * x) + 0.5


def _silu(x):
    return x * _sigmoid(x)


def _split_bf16(x, terms):
    out = []
    for _ in range(terms - 1):
        hi = x.astype(BF16)
        out.append(hi)
        x = x - hi.astype(F32)
    out.append(x.astype(BF16))
    return out


def _ada_kernel(c_ref, w_ref, b_ref, o_ref):
    a = _silu(c_ref[...])
    a_hi, a_lo = _split_bf16(a, 2)
    w_hi, w_lo = _split_bf16(w_ref[...], 2)
    acc = _dot(a_hi, w_hi) + _dot(a_lo, w_hi) + _dot(a_hi, w_lo)
    o_ref[...] = acc + b_ref[...]


def _ada(cond, w_ada, b_ada):
    depth, d, n = w_ada.shape
    tn = _tile(n, 512)
    rows = cond.shape[0]
    return pl.pallas_call(
        _ada_kernel,
        grid=(depth, n // tn),
        in_specs=[
            pl.BlockSpec((rows, d), lambda l, j: (0, 0)),
            pl.BlockSpec((None, d, tn), lambda l, j: (l, 0, j)),
            pl.BlockSpec((None, 1, tn), lambda l, j: (l, 0, j)),
        ],
        out_specs=pl.BlockSpec((None, rows, tn), lambda l, j: (l, 0, j)),
        out_shape=jax.ShapeDtypeStruct((depth, rows, n), F32),
        compiler_params=_params("parallel", "parallel"),
        name="ada_mod",
    )(cond, w_ada, b_ada.reshape(depth, 1, n))


def _in_proj_kernel(x_ref, shift_ref, scale_ref, gain_ref, w_ref, o_ref, h_ref):
    @pl.when(pl.program_id(1) == 0)
    def _():
        x = x_ref[...]
        ms = jnp.mean(x * x, axis=-1, keepdims=True)
        y = x * lax.rsqrt(ms + NORM_EPS) * gain_ref[...]
        h_ref[...] = (y * (1.0 + scale_ref[...]) + shift_ref[...]).astype(BF16)

    o_ref[...] = _dot(h_ref[...], w_ref[...])


def _in_proj(x2d, shift, scale, gain, w_in, layer, rows_per_cond):
    m, d = x2d.shape
    n = w_in.shape[-1]
    tm = _tile(rows_per_cond, 1024)
    tn = _tile(n, 1024)
    return pl.pallas_call(
        _in_proj_kernel,
        grid=(m // tm, n // tn),
        in_specs=[
            pl.BlockSpec((tm, d), lambda i, j: (i, 0)),
            pl.BlockSpec((None, 1, d), lambda i, j: (i * tm // rows_per_cond, 0, 0)),
            pl.BlockSpec((None, 1, d), lambda i, j: (i * tm // rows_per_cond, 0, 0)),
            pl.BlockSpec((1, d), lambda i, j: (0, 0)),
            pl.BlockSpec((None, d, tn), lambda i, j: (layer, 0, j)),
        ],
        out_specs=pl.BlockSpec((tm, tn), lambda i, j: (i, j)),
        out_shape=jax.ShapeDtypeStruct((m, n), F32),
        scratch_shapes=[pltpu.VMEM((tm, d), BF16)],
        compiler_params=_params("parallel", "arbitrary"),
        name="in_proj",
    )(x2d, shift, scale, gain, w_in)


def _head_norm(x, gain, group):
    x2 = x * x
    if group == LANES:
        ms = jnp.sum(x2, axis=-1, keepdims=True) * (1.0 / LANES)
    else:
        lo = lax.broadcasted_iota(jnp.int32, x.shape, 1) < group
        s_lo = jnp.sum(jnp.where(lo, x2, 0.0), axis=-1, keepdims=True)
        s_hi = jnp.sum(jnp.where(lo, 0.0, x2), axis=-1, keepdims=True)
        ms = jnp.where(lo, s_lo, s_hi) * (1.0 / group)
    return x * lax.rsqrt(ms + NORM_EPS) * gain


PREP_W = 2 * LANES


def _group_mean_matrix(group):
    i = np.arange(PREP_W)
    return ((i[:, None] // group) == (i[None, :] // group)).astype(np.float32) / group


def _rotate_half_matrix(d):
    quarter = d // 4
    i = np.arange(PREP_W)
    first = (i % (2 * quarter)) < quarter
    mat = np.zeros((PREP_W, PREP_W), np.float32)
    mat[i[first] + quarter, i[first]] = -1.0
    mat[i[~first] - quarter, i[~first]] = 1.0
    return mat


def _prep_kernel(*refs, rope, keep_f32):
    it = iter(refs)
    dq_ref, dk_ref, dv_ref, wq_ref, wk_ref, wv_ref = (next(it) for _ in range(6))
    gdq_ref, gdk_ref, gwq_ref, gwk_ref, md_ref, mw_ref = (next(it) for _ in range(6))
    if rope:
        rd_ref, rw_ref, cd_ref, sd_ref, cw_ref, sw_ref = (next(it) for _ in range(6))
    if keep_f32:
        for _ in range(4):
            next(it)
    odq, odk, odv, owq, owk, owv = (next(it) for _ in range(6))
    if keep_f32:
        odk32, odv32, owk32, owv32 = (next(it) for _ in range(4))

    def pair(src, j, gain_ref, mean_ref, rot_ref, cos_ref, sin_ref):
        x = src[:, j * PREP_W:(j + 1) * PREP_W]
        hi, lo = _split_bf16(x * x, 2)
        ms = _dot(hi, mean_ref[...]) + _dot(lo, mean_ref[...])
        y = x * lax.rsqrt(ms + NORM_EPS) * gain_ref[...]
        if rope:
            y = y * cos_ref[...] + _dot(y.astype(BF16), rot_ref[...]) * sin_ref[...]
        return y

    dtabs = (rd_ref, cd_ref, sd_ref) if rope else (None, None, None)
    wtabs = (rw_ref, cw_ref, sw_ref) if rope else (None, None, None)
    for j in range(N_HEADS // 2):
        sl = slice(j * PREP_W, (j + 1) * PREP_W)
        q = pair(dq_ref, j, gdq_ref, md_ref, *dtabs)
        odq[:, sl] = (q * (DIFF_QK_DIM ** -0.5 * LOG2E)).astype(BF16)
        k = pair(dk_ref, j, gdk_ref, md_ref, *dtabs)
        odk[:, sl] = k.astype(BF16)
        if keep_f32:
            odk32[:, sl] = k
        q = pair(wq_ref, j, gwq_ref, mw_ref, *wtabs)
        owq[:, sl] = (q * (LANES ** -0.5 * LOG2E)).astype(BF16)
    for j in range(WIN_KV_HEADS // 2):
        sl = slice(j * PREP_W, (j + 1) * PREP_W)
        k = pair(wk_ref, j, gwk_ref, mw_ref, *wtabs)
        owk[:, sl] = k.astype(BF16)
        if keep_f32:
            owk32[:, sl] = k
    for j in range(N_HEADS):
        odv[j] = dv_ref[:, j * LANES:(j + 1) * LANES].T.astype(BF16)
    for j in range(WIN_KV_HEADS):
        for u in range(owv.shape[1]):
            owv[j, u] = wv_ref[u * WINDOW:(u + 1) * WINDOW, j * LANES:(j + 1) * LANES].T.astype(BF16)
    if keep_f32:
        odv32[...] = dv_ref[...]
        owv32[...] = wv_ref[...]


def _prep(proj, gains, rope_tabs, seq_len, caches=None, layer=0):
    m = proj.shape[0]
    keep_f32 = caches is not None
    tr = _tile(seq_len, DIFF_KV_CHUNK)
    nt = seq_len // tr
    w8, w4 = BRANCH_W, WIN_KV_HEADS * LANES
    rope = rope_tabs is not None

    def col(width, off):
        return pl.BlockSpec((tr, width), lambda i: (i, off // width))

    small = pl.BlockSpec((1, PREP_W), lambda i: (0, 0))
    square = pl.BlockSpec((PREP_W, PREP_W), lambda i: (0, 0))
    in_specs = [col(w8, OFF_DQ), col(w8, OFF_DK), col(w8, OFF_DV), col(w8, OFF_WQ),
                col(w4, OFF_WK), col(w4, OFF_WV), small, small, small, small, square, square]
    args = [proj] * 6 + list(gains) + [jnp.asarray(_group_mean_matrix(DIFF_QK_DIM), BF16),
                                        jnp.asarray(_group_mean_matrix(LANES), BF16)]
    if rope:
        tab = pl.BlockSpec((tr, PREP_W), lambda i: (i % nt, 0))
        in_specs += [square, square] + [tab] * 4
        args += [jnp.asarray(_rotate_half_matrix(DIFF_QK_DIM), BF16),
                 jnp.asarray(_rotate_half_matrix(LANES), BF16)] + list(rope_tabs)
    out8 = pl.BlockSpec((tr, w8), lambda i: (i, 0))
    out4 = pl.BlockSpec((tr, w4), lambda i: (i, 0))
    out_vt = pl.BlockSpec((None, N_HEADS, None, LANES, tr), lambda i: (i // nt, 0, i % nt, 0, 0))
    kb = tr // WINDOW
    out_wvt = pl.BlockSpec((None, WIN_KV_HEADS, kb, LANES, WINDOW), lambda i: (i // nt, 0, i % nt, 0, 0))
    out_specs = [out8, out8, out_vt, out8, out4, out_wvt]
    out_shape = [jax.ShapeDtypeStruct((m, w8), BF16), jax.ShapeDtypeStruct((m, w8), BF16),
                 jax.ShapeDtypeStruct((m // seq_len, N_HEADS, nt, LANES, tr), BF16),
                 jax.ShapeDtypeStruct((m, w8), BF16), jax.ShapeDtypeStruct((m, w4), BF16),
                 jax.ShapeDtypeStruct((m // seq_len, WIN_KV_HEADS, seq_len // WINDOW, LANES, WINDOW), BF16)]
    aliases = {}
    if keep_f32:
        for j, cache in enumerate(caches):
            aliases[len(args)] = len(out_shape)
            in_specs.append(pl.BlockSpec(memory_space=pl.ANY))
            args.append(cache)
            out_specs.append(pl.BlockSpec((None, None, tr, cache.shape[-1]),
                                          lambda i: (i // nt, layer, i % nt, 0)))
            out_shape.append(jax.ShapeDtypeStruct(cache.shape, F32))
    return pl.pallas_call(
        functools.partial(_prep_kernel, rope=rope, keep_f32=keep_f32),
        grid=(m // tr,),
        in_specs=in_specs,
        out_specs=out_specs,
        out_shape=out_shape,
        input_output_aliases=aliases,
        compiler_params=_params("parallel"),
        name="qk_prep",
    )(*args)


def _diff_attn_kernel(*refs, lam_init, n_chunks, ck, has_ctx):
    it = iter(refs)
    lam_ref, subln_ref, q_ref, k_ref, vt_ref = (next(it) for _ in range(5))
    if has_ctx:
        kc_ref, vct_ref = next(it), next(it)
    g_ref, o_ref, acc_ref = next(it), next(it), next(it)

    tq = q_ref.shape[0]
    lv = lam_ref[...]
    lam = (jnp.exp(jnp.sum(lv[0:1] * lv[1:2], axis=-1, keepdims=True))
           - jnp.exp(jnp.sum(lv[2:3] * lv[3:4], axis=-1, keepdims=True)) + lam_init)
    first = lax.broadcasted_iota(jnp.int32, (tq, LANES), 1) < DIFF_QK_DIM

    for hh in range(vt_ref.shape[0]):
        sl = slice(hh * LANES, (hh + 1) * LANES)
        q = q_ref[:, sl]
        zero = jnp.zeros_like(q)
        q2 = jnp.concatenate([jnp.where(first, q, zero), jnp.where(first, zero, q)], axis=0)

        def step(kb, vtb, carry, is_first, hh=hh, q2=q2):
            m, l = carry
            s = _dot_nt(kb, q2)
            m_new = jnp.maximum(m, jnp.max(s, axis=0, keepdims=True))
            p = jnp.exp2(s - m_new)
            pv = _dot(vtb, p.astype(BF16))
            if is_first:
                l = jnp.sum(p, axis=0, keepdims=True)
                acc_ref[hh] = pv
            else:
                alpha = jnp.exp2(m - m_new)
                l = alpha * l + jnp.sum(p, axis=0, keepdims=True)
                acc_ref[hh] = alpha * acc_ref[hh] + pv
            return m_new, l

        none = jnp.zeros((1, 2 * tq), F32)
        carry = step(k_ref[pl.ds(0, ck), sl], vt_ref[hh, 0], (none, none), True)

        def body(c, carry, hh=hh, sl=sl, step=step):
            r = pl.multiple_of(c * ck, ck)
            return step(k_ref[pl.ds(r, ck), sl], vt_ref[hh, c], carry, False)

        carry = lax.fori_loop(1, n_chunks, body, carry)
        if has_ctx:
            carry = step(kc_ref[:, sl], vct_ref[hh], carry, False)

        r = acc_ref[hh] / carry[1]
        o = (r[:, :tq] - lam * r[:, tq:]).T
        ms = jnp.mean(o * o, axis=-1, keepdims=True)
        y = o * lax.rsqrt(ms + NORM_EPS) * subln_ref[...]
        y = y * (1.0 - lam_init)
        o_ref[:, sl] = (y * _silu(g_ref[:, sl])).astype(BF16)


def _diff_attn(q, k, vt, ctx, proj3, lam_vecs, subln, lam_init):
    b, t, _ = q.shape
    n_chunks, ck = vt.shape[2], vt.shape[4]
    tq = _tile(t, DIFF_Q_TILE)
    hps = max(1, min(N_HEADS, DIFF_Q_TILE // t))
    hw = hps * LANES
    has_ctx = ctx is not None
    qspec = pl.BlockSpec((None, tq, hw), lambda bi, h, i: (bi, i, h))
    in_specs = [pl.BlockSpec((4, DIFF_QK_DIM), lambda bi, h, i: (0, 0)),
                pl.BlockSpec((1, LANES), lambda bi, h, i: (0, 0)),
                qspec,
                pl.BlockSpec((None, t, hw), lambda bi, h, i: (bi, 0, h)),
                pl.BlockSpec((None, hps, n_chunks, LANES, ck), lambda bi, h, i: (bi, h, 0, 0, 0))]
    args = [lam_vecs, subln, q, k, vt]
    if has_ctx:
        p = ctx[0].shape[1]
        in_specs += [pl.BlockSpec((None, p, hw), lambda bi, h, i: (bi, 0, h)),
                     pl.BlockSpec((None, hps, LANES, p), lambda bi, h, i: (bi, h, 0, 0))]
        args += list(ctx)
    in_specs.append(pl.BlockSpec((None, tq, hw), lambda bi, h, i: (bi, i, OFF_DG // hw + h)))
    args.append(proj3)
    return pl.pallas_call(
        functools.partial(_diff_attn_kernel, lam_init=lam_init, n_chunks=n_chunks, ck=ck, has_ctx=has_ctx),
        grid=(b, N_HEADS // hps, t // tq),
        in_specs=in_specs,
        out_specs=qspec,
        out_shape=jax.ShapeDtypeStruct((b, t, BRANCH_W), BF16),
        scratch_shapes=[pltpu.VMEM((hps, LANES, 2 * tq), F32)],
        compiler_params=_params("parallel", "parallel", "parallel"),
        name="diff_attn",
    )(*args)


def _win_attn_kernel(*refs, banded, n_blocks, blocks_per_step, has_ctx):
    it = iter(refs)
    sink_ref, q_ref, k_ref, vt_ref = (next(it) for _ in range(4))
    if has_ctx:
        kc_ref, vct_ref = next(it), next(it)
    g_ref, o_ref = next(it), next(it)

    kvh = pl.program_id(1)
    w = WINDOW
    gw = WIN_GROUP * w
    lane = lax.broadcasted_iota(jnp.int32, (1, gw), 1)
    sink = jnp.where(lane < w, sink_ref[kvh * WIN_GROUP], sink_ref[kvh * WIN_GROUP + 1]) * LOG2E
    if banded:
        key = lax.broadcasted_iota(jnp.int32, (w, gw), 0)
        qry = lax.broadcasted_iota(jnp.int32, (w, gw), 1) & (w - 1)
    shared = []
    if not banded:
        shared += [(k_ref[j * w:(j + 1) * w, :], vt_ref[j], None) for j in range(n_blocks)]

    q2s = [jnp.concatenate([q_ref[u * w:(u + 1) * w, j * LANES:(j + 1) * LANES] for j in range(WIN_GROUP)],
                           axis=0) for u in range(blocks_per_step)]
    if has_ctx:
        s_ctx = _dot_nt(kc_ref[...], jnp.concatenate(q2s, axis=0))
    stats, p_ctx = [], []

    for u in range(blocks_per_step):
        cols = slice(u * gw, (u + 1) * gw)
        q2 = q2s[u]
        segs = []
        if banded:
            n = pl.program_id(2) * blocks_per_step + u
            b_prev = jnp.maximum(n - 1, 0)
            b_next = jnp.minimum(n + 1, n_blocks - 1)
            keep_prev = (key >= qry) & (n > 0)
            keep_next = (key <= qry) & (n < n_blocks - 1)
            segs = [(k_ref[pl.ds(pl.multiple_of(b_prev * w, w), w), :], vt_ref[b_prev], keep_prev),
                    (k_ref[pl.ds(pl.multiple_of(n * w, w), w), :], vt_ref[n], None),
                    (k_ref[pl.ds(pl.multiple_of(b_next * w, w), w), :], vt_ref[b_next], keep_next)]
        segs += shared
        scores = []
        for kb, _, keep in segs:
            s = _dot_nt(kb, q2)
            scores.append(s if keep is None else jnp.where(keep, s, -jnp.inf))
        m = sink
        for s in scores:
            m = jnp.maximum(m, jnp.max(s, axis=0, keepdims=True))
        if has_ctx:
            m = jnp.maximum(m, jnp.max(s_ctx[:, cols], axis=0, keepdims=True))
        l = jnp.exp2(sink - m)
        acc = None
        for s, (_, vtb, _) in zip(scores, segs):
            p = jnp.exp2(s - m)
            l = l + jnp.sum(p, axis=0, keepdims=True)
            pv = _dot(vtb, p.astype(BF16))
            acc = pv if acc is None else acc + pv
        if has_ctx:
            p = jnp.exp2(s_ctx[:, cols] - m)
            l = l + jnp.sum(p, axis=0, keepdims=True)
            p_ctx.append(p.astype(BF16))
        stats.append((acc, l))

    if has_ctx:
        pv_ctx = _dot(vct_ref[...], jnp.concatenate(p_ctx, axis=1))
    for u in range(blocks_per_step):
        rows = slice(u * w, (u + 1) * w)
        acc, l = stats[u]
        if has_ctx:
            acc = acc + pv_ctx[:, u * gw:(u + 1) * gw]
        o = (acc / l).T
        for j in range(WIN_GROUP):
            sl = slice(j * LANES, (j + 1) * LANES)
            o_ref[rows, sl] = (o[j * w:(j + 1) * w] * _silu(g_ref[rows, sl])).astype(BF16)


def _win_attn(q, k, vt, ctx, proj3, sink, banded):
    b, t, _ = q.shape
    nb = t // WINDOW
    gw = WIN_GROUP * LANES
    has_ctx = ctx is not None
    per_step = _tile(nb, WIN_BLOCKS_PER_STEP)
    tq = per_step * WINDOW
    qspec = pl.BlockSpec((None, tq, gw), lambda bi, h, i: (bi, i, h))
    in_specs = [pl.BlockSpec(memory_space=pltpu.SMEM), qspec,
                pl.BlockSpec((None, t, LANES), lambda bi, h, i: (bi, 0, h)),
                pl.BlockSpec((None, None, nb, LANES, WINDOW), lambda bi, h, i: (bi, h, 0, 0, 0))]
    args = [sink, q, k, vt]
    if has_ctx:
        p = ctx[0].shape[1]
        in_specs += [pl.BlockSpec((None, p, LANES), lambda bi, h, i: (bi, 0, h)),
                     pl.BlockSpec((None, None, LANES, p), lambda bi, h, i: (bi, h, 0, 0))]
        args += list(ctx)
    in_specs.append(pl.BlockSpec((None, tq, gw), lambda bi, h, i: (bi, i, OFF_WG // gw + h)))
    args.append(proj3)
    return pl.pallas_call(
        functools.partial(_win_attn_kernel, banded=banded, n_blocks=nb, blocks_per_step=per_step,
                          has_ctx=has_ctx),
        grid=(b, WIN_KV_HEADS, nb // per_step),
        in_specs=in_specs,
        out_specs=qspec,
        out_shape=jax.ShapeDtypeStruct((b, t, BRANCH_W), BF16),
        compiler_params=_params("parallel", "parallel", "parallel"),
        name="win_attn",
    )(*args)


def _hgrn_tables(reverse):
    c = HG_CHUNK
    t = np.arange(c)[:, None]
    u = np.arange(c)[None, :]
    mats, masks = [], []
    for m in range(HG_LEVELS):
        half, blk = 1 << m, 2 << m
        r = t - t % blk + half - 1
        right = (t % blk) >= half
        if not reverse:
            e_q = (u >= r + 1) & (u <= t)
            e_k = (u >= t + 1) & (u <= r)
            mats.append(np.where(right, e_q, e_k))
            masks.append(right & ~right.T & (t // blk == u // blk))
        else:
            e_q = (u >= t) & (u <= r)
            e_k = (u >= r + 1) & (u <= t - 1)
            mats.append(np.where(right, e_k, e_q))
            masks.append(~right & right.T & (t // blk == u // blk))
    mats = mats[:HG_MXU_LEVELS] + [(u <= t) if not reverse else (u >= t)]
    masks.append(t == u)
    return (np.concatenate(mats, 0).astype(np.float32), np.stack(masks).astype(np.float32))


def _hgrn_constants():
    c = HG_CHUNK
    tabs = [_hgrn_tables(False), _hgrn_tables(True)]
    mats = np.stack([np.tile(tb[0], (1, 3)) for tb in tabs])
    masks = np.zeros((HG_LEVELS + 1, 4, c, LANES), np.float32)
    for a in range(4):
        lo = (a % 2) * c
        masks[:, a, :, lo:lo + c] = tabs[a // 2][1]
    return mats, masks


def _hgrn_kernel(lb_ref, s0_ref, mat_ref, mask_ref,
                 qf_ref, vf_ref, zf_ref, qb_ref, vb_ref, zb_ref, *rest, layer, n_chunks):
    of_ref, ob_ref, sout_ref, st_ref = rest[-4:]
    i = pl.program_id(2)
    c = HG_CHUNK
    pair = 2
    pw = pair * LANES

    @pl.when(i == 0)
    def _():
        for d in range(2):
            for hh in range(pair):
                st_ref[2 * d + hh] = s0_ref[d, hh]

    def lower_bound(d):
        rows = [lb_ref[d, r:r + 1, :] for r in range(lb_ref.shape[1])]
        mx = functools.reduce(jnp.maximum, rows)
        es = [jnp.exp(r - mx) for r in rows]
        den = functools.reduce(lambda a, b_: a + b_, es)
        ps = [e / den for e in es]
        return functools.reduce(lambda a, b_: a + b_, ps[:layer + 1]) - ps[0]

    bit_row = lax.broadcasted_iota(jnp.int32, (c, pw), 0)

    def side_rows(m, d, on_query_rows, on_key_rows):
        half = 1 << m
        if half % 8:
            return jnp.where(((bit_row >> m) & 1) == (1 - d), on_query_rows, on_key_rows)
        slabs = [(on_query_rows if (r0 // half) % 2 == 1 - d else on_key_rows)[r0:r0 + half]
                 for r0 in range(0, c, half)]
        return jnp.concatenate(slabs, axis=0)

    def decays(d, z, lb):
        e = jnp.exp(-jnp.abs(z))
        r = 1.0 / (1.0 + e)
        er = e * r
        pos = z >= 0.0
        f = lb + (1.0 - lb) * jnp.where(pos, r, er)
        kk = (1.0 - lb) * jnp.where(pos, er, r)
        logf = jnp.log(jnp.maximum(f, F_FLOOR)) * LOG2E
        parts = jnp.concatenate(_split_bf16(logf, 3), axis=0)
        e_small = _dot(mat_ref[d], parts)
        run = e_small[HG_MXU_LEVELS * c:]
        levels = [jnp.exp2(e_small[m * c:(m + 1) * c]) for m in range(HG_MXU_LEVELS)]
        for m in range(HG_MXU_LEVELS, HG_LEVELS):
            half, blk = 1 << m, 2 << m
            edge = half - 1 + d
            rho = jnp.concatenate(
                [jnp.broadcast_to(run[b0 + edge:b0 + edge + 1], (blk, pw)) for b0 in range(0, c, blk)], axis=0)
            diff = run - rho
            levels.append(jnp.exp2(side_rows(m, d, diff, -diff)))
        last = (c - 1) if d == 0 else 0
        total = run[last:last + 1]
        return kk, levels, jnp.exp2(run), jnp.exp2(total - run), jnp.exp2(total)

    lbs = (lower_bound(0), lower_bound(1))

    def body(ci, _):
        rows = (pl.ds(pl.multiple_of(ci * c, c), c), pl.ds(pl.multiple_of((n_chunks - 1 - ci) * c, c), c))
        qs = (qf_ref[rows[0], :], qb_ref[rows[1], :])
        vs = (vf_ref[rows[0], :].astype(BF16), vb_ref[rows[1], :].astype(BF16))
        zs = (zf_ref[rows[0], :], zb_ref[rows[1], :])
        dec = [decays(d, zs[d], lbs[d]) for d in range(2)]

        def stacked(fn):
            wide = [fn(d).astype(BF16) for d in range(2)]
            return jnp.concatenate([wide[a // 2][:, (a % 2) * LANES:(a % 2 + 1) * LANES] for a in range(4)],
                                   axis=0)

        def pair_weights(g, m):
            return [mask_ref[m, a] * g[a * c:(a + 1) * c, (a // 2) * LANES:(a // 2 + 1) * LANES]
                    for a in range(4)]

        acc = pair_weights(_dot_nt(stacked(lambda d: qs[d]), stacked(lambda d: dec[d][0])), HG_LEVELS)
        for m in range(HG_LEVELS):
            def level_rows(d, m=m):
                return side_rows(m, d, qs[d], dec[d][0]) * dec[d][1][m]
            lm = stacked(level_rows)
            acc = [x + y for x, y in zip(acc, pair_weights(_dot_nt(lm, lm), m))]

        outs = (of_ref, ob_ref)
        for a in range(4):
            d, hh = a // 2, a % 2
            sl = slice(hh * LANES, (hh + 1) * LANES)
            kk, _, from_start, to_end, whole = dec[d]
            v = vs[d][:, sl]
            qe = (qs[d][:, sl] * from_start[:, sl]).astype(BF16)
            ke = (kk[:, sl] * to_end[:, sl]).astype(BF16)
            st = st_ref[a]
            lhs = jnp.concatenate([acc[a].astype(BF16), qe], axis=1)
            rhs = jnp.concatenate([v, v, st.astype(BF16)], axis=0)
            outs[d][rows[d], sl] = _dot(lhs, rhs)
            decay_col = jnp.broadcast_to(whole[:, sl], (LANES, LANES)).T
            st_ref[a] = st * decay_col + lax.dot_general(ke, v, TN_DIMS, preferred_element_type=F32)
        return 0

    lax.fori_loop(0, n_chunks, body, 0, unroll=4)

    @pl.when(i == pl.num_programs(2) - 1)
    def _():
        for d in range(2):
            for hh in range(pair):
                sout_ref[d, hh] = st_ref[2 * d + hh]


def _hgrn(proj3, hg_lb, state0, layer, states_out=None):
    b, t, _ = proj3.shape
    tb = _tile(t, 512)
    nt = t // tb
    pw = 2 * LANES
    mats, masks = _hgrn_constants()
    mats = jnp.asarray(mats, BF16)
    masks = jnp.asarray(masks, F32)

    def fwd(off):
        return pl.BlockSpec((None, tb, pw), lambda bi, hp, i: (bi, i, off // pw + hp))

    def bwd(off):
        return pl.BlockSpec((None, tb, pw), lambda bi, hp, i: (bi, nt - 1 - i, off // pw + hp))

    st_spec = pl.BlockSpec((None, 2, 2, LANES, LANES), lambda bi, hp, i: (bi, 0, hp, 0, 0))
    in_specs = [
        pl.BlockSpec((2, hg_lb.shape[1], pw), lambda bi, hp, i: (0, 0, hp)),
        st_spec,
        pl.BlockSpec(mats.shape, lambda bi, hp, i: (0, 0, 0)),
        pl.BlockSpec(masks.shape, lambda bi, hp, i: (0, 0, 0, 0)),
        fwd(OFF_HQ), fwd(OFF_HI), fwd(OFF_HFF),
        bwd(OFF_HQ), bwd(OFF_HI), bwd(OFF_HFB),
    ]
    args = [hg_lb, state0, mats, masks, proj3, proj3, proj3, proj3, proj3, proj3]
    aliases = {}
    if states_out is None:
        st_out_spec, st_out_shape = st_spec, jax.ShapeDtypeStruct(state0.shape, F32)
    else:
        aliases[len(args)] = 2
        in_specs.append(pl.BlockSpec(memory_space=pl.ANY))
        args.append(states_out)
        st_out_spec = pl.BlockSpec((None, None, 2, 2, LANES, LANES), lambda bi, hp, i: (bi, layer, 0, hp, 0, 0))
        st_out_shape = jax.ShapeDtypeStruct(states_out.shape, F32)
    return pl.pallas_call(
        functools.partial(_hgrn_kernel, layer=layer, n_chunks=tb // HG_CHUNK),
        grid=(b, N_HEADS // 2, nt),
        in_specs=in_specs,
        out_specs=[
            pl.BlockSpec((None, tb, pw), lambda bi, hp, i: (bi, i, hp)),
            pl.BlockSpec((None, tb, pw), lambda bi, hp, i: (bi, nt - 1 - i, hp)),
            st_out_spec,
        ],
        out_shape=[
            jax.ShapeDtypeStruct((b, t, BRANCH_W), F32),
            jax.ShapeDtypeStruct((b, t, BRANCH_W), F32),
            st_out_shape,
        ],
        scratch_shapes=[pltpu.VMEM((4, LANES, LANES), F32)],
        input_output_aliases=aliases,
        compiler_params=_params("parallel", "parallel", "arbitrary"),
        name="hgrn_scan",
    )(*args)


def _merge_kernel(yd_ref, of_ref, ob_ref, yw_ref, hg_ref, gain_ref, mg0_ref, mg1_ref, mg2_ref, w_ref,
                  o_ref, yh_ref):
    @pl.when(pl.program_id(1) == 0)
    def _():
        for j in range(N_HEADS):
            sl = slice(j * LANES, (j + 1) * LANES)
            y = _head_norm(of_ref[:, sl] + ob_ref[:, sl], gain_ref[...], LANES)
            yh_ref[:, sl] = (y * _silu(hg_ref[:, sl])).astype(BF16)

    acc = _sigmoid(mg0_ref[...]) * _dot(yd_ref[...], w_ref[0])
    acc = acc + _sigmoid(mg1_ref[...]) * _dot(yh_ref[...], w_ref[1])
    acc = acc + _sigmoid(mg2_ref[...]) * _dot(yw_ref[...], w_ref[2])
    o_ref[...] = acc.astype(BF16)


def _merge(yd, o_f, o_b, yw, proj, hg_gain, w_branch, layer):
    m = yd.shape[0]
    d = w_branch.shape[-1]
    tm = _tile(m, 1024)
    tn = _tile(d, 256)
    row = lambda width: pl.BlockSpec((tm, width), lambda i, j: (i, 0))

    def mg(nb):
        return pl.BlockSpec((tm, tn), lambda i, j: (i, (OFF_MG + nb * d) // tn + j))

    return pl.pallas_call(
        _merge_kernel,
        grid=(m // tm, d // tn),
        in_specs=[row(BRANCH_W), row(BRANCH_W), row(BRANCH_W), row(BRANCH_W),
                  pl.BlockSpec((tm, BRANCH_W), lambda i, j: (i, OFF_HG // BRANCH_W)),
                  pl.BlockSpec((1, LANES), lambda i, j: (0, 0)),
                  mg(0), mg(1), mg(2),
                  pl.BlockSpec((None, 3, BRANCH_W, tn), lambda i, j: (layer, 0, 0, j))],
        out_specs=pl.BlockSpec((tm, tn), lambda i, j: (i, j)),
        out_shape=jax.ShapeDtypeStruct((m, d), BF16),
        scratch_shapes=[pltpu.VMEM((tm, BRANCH_W), BF16)],
        compiler_params=_params("parallel", "arbitrary"),
        name="branch_merge",
    )(yd, o_f, o_b, yw, proj, hg_gain, proj, proj, proj, w_branch)


def _out_proj_kernel(x_ref, gate_ref, m_ref, w_ref, o_ref):
    o_ref[...] = x_ref[...] + gate_ref[...] * _dot(m_ref[...], w_ref[...])


def _out_proj(x2d, gate, merged, w_out, layer, rows_per_cond):
    m, d = x2d.shape
    tm = _tile(rows_per_cond, 1024)
    tn = _tile(d, 512)
    return pl.pallas_call(
        _out_proj_kernel,
        grid=(m // tm, d // tn),
        in_specs=[
            pl.BlockSpec((tm, tn), lambda i, j: (i, j)),
            pl.BlockSpec((None, 1, tn), lambda i, j: (i * tm // rows_per_cond, 0, j)),
            pl.BlockSpec((tm, d), lambda i, j: (i, 0)),
            pl.BlockSpec((None, d, tn), lambda i, j: (layer, 0, j)),
        ],
        out_specs=pl.BlockSpec((tm, tn), lambda i, j: (i, j)),
        out_shape=jax.ShapeDtypeStruct((m, d), F32),
        compiler_params=_params("parallel", "parallel"),
        name="out_proj",
    )(x2d, gate, merged, w_out)


def _rope_tables(t, d):
    rows = t // GRID_W
    row = jnp.repeat(jnp.arange(rows, dtype=F32), GRID_W)
    col = jnp.tile(jnp.arange(GRID_W, dtype=F32), rows)
    quarter = d // 4
    inv = ROPE_BASE ** (-jnp.arange(quarter, dtype=F32) / quarter)
    ar = row[:, None] * inv
    ac = col[:, None] * inv
    ang = jnp.concatenate([ar, ar, ac, ac], axis=-1)
    return tuple(jnp.tile(a, (1, PREP_W // d)) for a in (jnp.cos(ang), jnp.sin(ang)))


def _tile_gain(g, width=LANES):
    return jnp.tile(g, width // g.shape[0]).reshape(1, width)


def _layer(x, mod, l, w, rope_tabs, caches, state0, new_caches):
    is_context = new_caches is not None
    b, t, d = x.shape
    m = b * t
    rows_per_cond = m // mod.shape[0]
    x2d = x.reshape(m, d)
    shift, scale, gate = (mod[:, None, j * d:(j + 1) * d] for j in range(3))
    proj = _in_proj(x2d, shift, scale, w["norm_gain"][l].reshape(1, d), w["w_in"], l, rows_per_cond)
    gains = tuple(_tile_gain(w[name][l], PREP_W)
                  for name in ("diff_q_norm", "diff_k_norm", "win_q_norm", "win_k_norm"))
    prepped = _prep(proj, gains, rope_tabs, t, new_caches[:4] if is_context else None, l)
    dq, dk, wq, wk = (prepped[j].reshape(b, t, -1) for j in (0, 1, 3, 4))
    dv, wv = prepped[2], prepped[5]
    proj3 = proj.reshape(b, t, -1)
    lam_init = 0.8 - 0.6 * math.exp(-0.3 * l)
    if is_context:
        dctx = wctx = None
    else:
        ck_d, cv_d, ck_w, cv_w = caches
        dctx = (ck_d.reshape(b, -1, BRANCH_W).astype(BF16),
                jnp.transpose(cv_d, (0, 2, 3, 1)).astype(BF16))
        wctx = (ck_w.reshape(b, -1, WIN_KV_HEADS * LANES).astype(BF16),
                jnp.transpose(cv_w, (0, 2, 3, 1)).astype(BF16))
    yd = _diff_attn(dq, dk, dv, dctx, proj3, w["diff_lambda"][l], _tile_gain(w["diff_subln"][l]), lam_init)
    o_f, o_b, s_out = _hgrn(proj3, w["hg_lb"], state0, l, new_caches[4] if is_context else None)
    yw = _win_attn(wq, wk, wv, wctx, proj3, w["win_sink"][l], banded=not is_context)
    merged = _merge(yd.reshape(m, -1), o_f.reshape(m, -1), o_b.reshape(m, -1), yw.reshape(m, -1), proj,
                    _tile_gain(w["hg_out_norm"][l]), w["w_branch"], l)
    y = _out_proj(x2d, gate, merged, w["w_out"], l, rows_per_cond).reshape(b, t, d)
    if not is_context:
        return y, None
    return y, tuple(prepped[6:10]) + (s_out,)


def kernel(x_prompt, x_sample, cache_diff_k, cache_diff_v, cache_win_k, cache_win_v, state_hgrn, c, c_ctx, norm_gain, w_ada, b_ada, w_in, diff_q_norm, diff_k_norm, diff_lambda, diff_subln, hg_lb, hg_out_norm, win_q_norm, win_k_norm, win_sink, w_branch, w_out):
    depth = w_in.shape[0]
    dec_b, dec_t, d = x_sample.shape
    w = {"norm_gain": norm_gain, "w_in": w_in.astype(BF16), "diff_q_norm": diff_q_norm,
         "diff_k_norm": diff_k_norm, "diff_lambda": diff_lambda, "diff_subln": diff_subln,
         "hg_lb": hg_lb, "hg_out_norm": hg_out_norm, "win_q_norm": win_q_norm, "win_k_norm": win_k_norm,
         "win_sink": win_sink, "w_branch": w_branch.astype(BF16), "w_out": w_out.astype(BF16)}

    cond_rows = 8 * ((1 + dec_b + 7) // 8)
    cond = jnp.zeros((cond_rows, d), F32).at[0].set(c_ctx).at[1:1 + dec_b].set(c)
    mod = _ada(cond, w_ada, b_ada)

    y_prompt = x_prompt
    pb, pt = x_prompt.shape[:2]
    zero_state = jnp.zeros((pb,) + state_hgrn.shape[2:], F32)
    new = tuple(jnp.zeros((pb, depth, pt, width), F32)
                for width in (BRANCH_W, BRANCH_W, WIN_KV_HEADS * LANES, WIN_KV_HEADS * LANES))
    new += (jnp.zeros((pb, depth) + state_hgrn.shape[2:], F32),)
    for l in range(depth):
        y_prompt, new = _layer(y_prompt, mod[l, 0:1], l, w, None, None, zero_state, new)

    ropes = _rope_tables(dec_t, DIFF_QK_DIM) + _rope_tables(dec_t, LANES)
    y_sample = x_sample
    for l in range(depth):
        caches = (cache_diff_k[:, l], cache_diff_v[:, l], cache_win_k[:, l], cache_win_v[:, l])
        y_sample, _ = _layer(y_sample, mod[l, 1:1 + dec_b], l, w, ropes, caches, state_hgrn[:, l], None)

    return (y_prompt, y_sample,
            new[0].reshape(pb, depth, pt, N_HEADS, 2, DIFF_QK_DIM),
            new[1].reshape(pb, depth, pt, N_HEADS, LANES),
            new[2].reshape(pb, depth, pt, WIN_KV_HEADS, LANES),
            new[3].reshape(pb, depth, pt, WIN_KV_HEADS, LANES),
            new[4])
```

```python
import functools
import math

import numpy as np
import jax
import jax.numpy as jnp
from jax import lax
from jax.experimental import pallas as pl
from jax.experimental.pallas import tpu as pltpu

F32 = jnp.float32
BF16 = jnp.bfloat16

NORM_EPS = 1e-6
F_FLOOR = 1e-30
ROPE_BASE = 10000.0
GRID_W = 64
WINDOW = 128

LANES = 128
N_HEADS = 8
WIN_KV_HEADS = 4
WIN_GROUP = 2
DIFF_QK_DIM = 64
BRANCH_W = N_HEADS * LANES
HG_CHUNK = 64
HG_LEVELS = 6
HG_MXU_LEVELS = 3
DIFF_KV_CHUNK = 512
WIN_BLOCKS_PER_STEP = 8
DIFF_Q_TILE = 2048
LOG2E = math.log2(math.e)

OFF_DQ, OFF_DK, OFF_DV, OFF_DG = 0, 1024, 2048, 3072
OFF_HQ, OFF_HFF, OFF_HFB, OFF_HI, OFF_HG = 4096, 5120, 6144, 7168, 8192
OFF_WQ, OFF_WK, OFF_WV, OFF_WG = 9216, 10240, 10752, 11264
OFF_MG = 12288

VMEM_LIMIT = 56 * 1024 * 1024

NT_DIMS = (((1,), (1,)), ((), ()))
TN_DIMS = (((0,), (0,)), ((), ()))


def _params(*sem):
    return pltpu.CompilerParams(dimension_semantics=sem, vmem_limit_bytes=VMEM_LIMIT)


def _tile(n, preferred):
    t = min(n, preferred)
    while n % t:
        t //= 2
    return t


def _dot(a, b):
    return jnp.dot(a, b, preferred_element_type=F32)


def _dot_nt(a, b):
    return lax.dot_general(a, b, NT_DIMS, preferred_element_type=F32)


def _sigmoid(x):
    return 0.5 * jnp.tanh(0.5 * x) + 0.5


def _silu(x):
    return x * _sigmoid(x)


def _split_bf16(x, terms):
    out = []
    for _ in range(terms - 1):
        hi = x.astype(BF16)
        out.append(hi)
        x = x - hi.astype(F32)
    out.append(x.astype(BF16))
    return out


def _ada_kernel(c_ref, w_ref, b_ref, o_ref):
    a = _silu(c_ref[...])
    a_hi, a_lo = _split_bf16(a, 2)
    w_hi, w_lo = _split_bf16(w_ref[...], 2)
    acc = _dot(a_hi, w_hi) + _dot(a_lo, w_hi) + _dot(a_hi, w_lo)
    o_ref[...] = acc + b_ref[...]


def _ada(cond, w_ada, b_ada):
    depth, d, n = w_ada.shape
    tn = _tile(n, 512)
    rows = cond.shape[0]
    return pl.pallas_call(
        _ada_kernel,
        grid=(depth, n // tn),
        in_specs=[
            pl.BlockSpec((rows, d), lambda l, j: (0, 0)),
            pl.BlockSpec((None, d, tn), lambda l, j: (l, 0, j)),
            pl.BlockSpec((None, 1, tn), lambda l, j: (l, 0, j)),
        ],
        out_specs=pl.BlockSpec((None, rows, tn), lambda l, j: (l, 0, j)),
        out_shape=jax.ShapeDtypeStruct((depth, rows, n), F32),
        compiler_params=_params("parallel", "parallel"),
        name="ada_mod",
    )(cond, w_ada, b_ada.reshape(depth, 1, n))


def _in_proj_kernel(x_ref, shift_ref, scale_ref, gain_ref, w_ref, o_ref, g_ref, h_ref, *, n_main):
    j = pl.program_id(1)

    @pl.when(j == 0)
    def _():
        x = x_ref[...]
        ms = jnp.mean(x * x, axis=-1, keepdims=True)
        y = x * lax.rsqrt(ms + NORM_EPS) * gain_ref[...]
        h_ref[...] = (y * (1.0 + scale_ref[...]) + shift_ref[...]).astype(BF16)
        g_ref[...] = jnp.zeros_like(g_ref)

    @pl.when(j < n_main)
    def _():
        o_ref[...] = _dot(h_ref[...], w_ref[...])

    @pl.when(j >= n_main)
    def _():
        g_ref[...] = _sigmoid(_dot(h_ref[...], w_ref[...])).astype(BF16)


def _in_proj(x2d, shift, scale, gain, w_in, layer, rows_per_cond):
    m, d = x2d.shape
    n = w_in.shape[-1]
    tm = _tile(rows_per_cond, 1024)
    tn = _tile(math.gcd(n, OFF_MG), 1024)
    n_main = OFF_MG // tn
    return pl.pallas_call(
        functools.partial(_in_proj_kernel, n_main=n_main),
        grid=(m // tm, n // tn),
        in_specs=[
            pl.BlockSpec((tm, d), lambda i, j: (i, 0)),
            pl.BlockSpec((None, 1, d), lambda i, j: (i * tm // rows_per_cond, 0, 0)),
            pl.BlockSpec((None, 1, d), lambda i, j: (i * tm // rows_per_cond, 0, 0)),
            pl.BlockSpec((1, d), lambda i, j: (0, 0)),
            pl.BlockSpec((None, d, tn), lambda i, j: (layer, 0, j)),
        ],
        out_specs=[pl.BlockSpec((tm, tn), lambda i, j: (i, jnp.minimum(j, n_main - 1))),
                   pl.BlockSpec((tm, tn), lambda i, j: (i, jnp.maximum(j - n_main, 0)))],
        out_shape=[jax.ShapeDtypeStruct((m, OFF_MG), F32), jax.ShapeDtypeStruct((m, n - OFF_MG), BF16)],
        scratch_shapes=[pltpu.VMEM((tm, d), BF16)],
        compiler_params=_params("parallel", "arbitrary"),
        name="in_proj",
    )(x2d, shift, scale, gain, w_in)


def _head_norm(x, gain, group):
    x2 = x * x
    if group == LANES:
        ms = jnp.sum(x2, axis=-1, keepdims=True) * (1.0 / LANES)
    else:
        lo = lax.broadcasted_iota(jnp.int32, x.shape, 1) < group
        s_lo = jnp.sum(jnp.where(lo, x2, 0.0), axis=-1, keepdims=True)
        s_hi = jnp.sum(jnp.where(lo, 0.0, x2), axis=-1, keepdims=True)
        ms = jnp.where(lo, s_lo, s_hi) * (1.0 / group)
    return x * lax.rsqrt(ms + NORM_EPS) * gain


PREP_W = 2 * LANES


def _group_mean_matrix(group):
    i = np.arange(PREP_W)
    return ((i[:, None] // group) == (i[None, :] // group)).astype(np.float32) / group


def _rotate_half_matrix(d):
    quarter = d // 4
    i = np.arange(PREP_W)
    first = (i % (2 * quarter)) < quarter
    mat = np.zeros((PREP_W, PREP_W), np.float32)
    mat[i[first] + quarter, i[first]] = -1.0
    mat[i[~first] - quarter, i[~first]] = 1.0
    return mat


def _prep_kernel(*refs, rope, keep_f32):
    it = iter(refs)
    dq_ref, dk_ref, dv_ref, wq_ref, wk_ref, wv_ref = (next(it) for _ in range(6))
    gdq_ref, gdk_ref, gwq_ref, gwk_ref, md_ref, mw_ref = (next(it) for _ in range(6))
    if rope:
        rd_ref, rw_ref, cd_ref, sd_ref, cw_ref, sw_ref = (next(it) for _ in range(6))
    if keep_f32:
        for _ in range(4):
            next(it)
    odq, odk, odv, owq, owk, owv = (next(it) for _ in range(6))
    if keep_f32:
        odk32, odv32, owk32, owv32 = (next(it) for _ in range(4))

    def pair(src, j, gain_ref, mean_ref, rot_ref, cos_ref, sin_ref):
        x = src[:, j * PREP_W:(j + 1) * PREP_W]
        hi, lo = _split_bf16(x * x, 2)
        ms = _dot(hi, mean_ref[...]) + _dot(lo, mean_ref[...])
        y = x * lax.rsqrt(ms + NORM_EPS) * gain_ref[...]
        if rope:
            y = y * cos_ref[...] + _dot(y.astype(BF16), rot_ref[...]) * sin_ref[...]
        return y

    dtabs = (rd_ref, cd_ref, sd_ref) if rope else (None, None, None)
    wtabs = (rw_ref, cw_ref, sw_ref) if rope else (None, None, None)
    for j in range(N_HEADS // 2):
        sl = slice(j * PREP_W, (j + 1) * PREP_W)
        q = pair(dq_ref, j, gdq_ref, md_ref, *dtabs)
        odq[:, sl] = (q * (DIFF_QK_DIM ** -0.5 * LOG2E)).astype(BF16)
        k = pair(dk_ref, j, gdk_ref, md_ref, *dtabs)
        odk[:, sl] = k.astype(BF16)
        if keep_f32:
            odk32[:, sl] = k
        q = pair(wq_ref, j, gwq_ref, mw_ref, *wtabs)
        owq[:, sl] = (q * (LANES ** -0.5 * LOG2E)).astype(BF16)
    for j in range(WIN_KV_HEADS // 2):
        sl = slice(j * PREP_W, (j + 1) * PREP_W)
        k = pair(wk_ref, j, gwk_ref, mw_ref, *wtabs)
        owk[:, sl] = k.astype(BF16)
        if keep_f32:
            owk32[:, sl] = k
    for j in range(N_HEADS):
        odv[j] = dv_ref[:, j * LANES:(j + 1) * LANES].T.astype(BF16)
    for j in range(WIN_KV_HEADS):
        for u in range(owv.shape[1]):
            owv[j, u] = wv_ref[u * WINDOW:(u + 1) * WINDOW, j * LANES:(j + 1) * LANES].T.astype(BF16)
    if keep_f32:
        odv32[...] = dv_ref[...]
        owv32[...] = wv_ref[...]


def _prep(proj, gains, rope_tabs, seq_len, caches=None, layer=0):
    m = proj.shape[0]
    keep_f32 = caches is not None
    tr = _tile(seq_len, DIFF_KV_CHUNK)
    nt = seq_len // tr
    w8, w4 = BRANCH_W, WIN_KV_HEADS * LANES
    rope = rope_tabs is not None

    def col(width, off):
        return pl.BlockSpec((tr, width), lambda i: (i, off // width))

    small = pl.BlockSpec((1, PREP_W), lambda i: (0, 0))
    square = pl.BlockSpec((PREP_W, PREP_W), lambda i: (0, 0))
    in_specs = [col(w8, OFF_DQ), col(w8, OFF_DK), col(w8, OFF_DV), col(w8, OFF_WQ),
                col(w4, OFF_WK), col(w4, OFF_WV), small, small, small, small, square, square]
    args = [proj] * 6 + list(gains) + [jnp.asarray(_group_mean_matrix(DIFF_QK_DIM), BF16),
                                        jnp.asarray(_group_mean_matrix(LANES), BF16)]
    if rope:
        tab = pl.BlockSpec((tr, PREP_W), lambda i: (i % nt, 0))
        in_specs += [square, square] + [tab] * 4
        args += [jnp.asarray(_rotate_half_matrix(DIFF_QK_DIM), BF16),
                 jnp.asarray(_rotate_half_matrix(LANES), BF16)] + list(rope_tabs)
    out8 = pl.BlockSpec((tr, w8), lambda i: (i, 0))
    out4 = pl.BlockSpec((tr, w4), lambda i: (i, 0))
    out_vt = pl.BlockSpec((None, N_HEADS, None, LANES, tr), lambda i: (i // nt, 0, i % nt, 0, 0))
    kb = tr // WINDOW
    out_wvt = pl.BlockSpec((None, WIN_KV_HEADS, kb, LANES, WINDOW), lambda i: (i // nt, 0, i % nt, 0, 0))
    out_specs = [out8, out8, out_vt, out8, out4, out_wvt]
    out_shape = [jax.ShapeDtypeStruct((m, w8), BF16), jax.ShapeDtypeStruct((m, w8), BF16),
                 jax.ShapeDtypeStruct((m // seq_len, N_HEADS, nt, LANES, tr), BF16),
                 jax.ShapeDtypeStruct((m, w8), BF16), jax.ShapeDtypeStruct((m, w4), BF16),
                 jax.ShapeDtypeStruct((m // seq_len, WIN_KV_HEADS, seq_len // WINDOW, LANES, WINDOW), BF16)]
    aliases = {}
    if keep_f32:
        for j, cache in enumerate(caches):
            aliases[len(args)] = len(out_shape)
            in_specs.append(pl.BlockSpec(memory_space=pl.ANY))
            args.append(cache)
            out_specs.append(pl.BlockSpec((None, None, tr, cache.shape[-1]),
                                          lambda i: (i // nt, layer, i % nt, 0)))
            out_shape.append(jax.ShapeDtypeStruct(cache.shape, F32))
    return pl.pallas_call(
        functools.partial(_prep_kernel, rope=rope, keep_f32=keep_f32),
        grid=(m // tr,),
        in_specs=in_specs,
        out_specs=out_specs,
        out_shape=out_shape,
        input_output_aliases=aliases,
        compiler_params=_params("parallel"),
        name="qk_prep",
    )(*args)


def _diff_attn_kernel(*refs, lam_init, n_chunks, ck, has_ctx):
    it = iter(refs)
    lam_ref, subln_ref, q_ref, k_ref, vt_ref = (next(it) for _ in range(5))
    if has_ctx:
        kc_ref, vct_ref = next(it), next(it)
    g_ref, o_ref, acc_ref = next(it), next(it), next(it)

    tq = q_ref.shape[0]
    lv = lam_ref[...]
    lam = (jnp.exp(jnp.sum(lv[0:1] * lv[1:2], axis=-1, keepdims=True))
           - jnp.exp(jnp.sum(lv[2:3] * lv[3:4], axis=-1, keepdims=True)) + lam_init)
    first = lax.broadcasted_iota(jnp.int32, (tq, LANES), 1) < DIFF_QK_DIM

    for hh in range(vt_ref.shape[0]):
        sl = slice(hh * LANES, (hh + 1) * LANES)
        q = q_ref[:, sl]
        zero = jnp.zeros_like(q)
        q2 = jnp.concatenate([jnp.where(first, q, zero), jnp.where(first, zero, q)], axis=0)

        def step(kb, vtb, carry, is_first, hh=hh, q2=q2):
            m, l = carry
            s = _dot_nt(kb, q2)
            m_new = jnp.maximum(m, jnp.max(s, axis=0, keepdims=True))
            p = jnp.exp2(s - m_new)
            pv = _dot(vtb, p.astype(BF16))
            if is_first:
                l = jnp.sum(p, axis=0, keepdims=True)
                acc_ref[hh] = pv
            else:
                alpha = jnp.exp2(m - m_new)
                l = alpha * l + jnp.sum(p, axis=0, keepdims=True)
                acc_ref[hh] = alpha * acc_ref[hh] + pv
            return m_new, l

        none = jnp.zeros((1, 2 * tq), F32)
        carry = step(k_ref[pl.ds(0, ck), sl], vt_ref[hh, 0], (none, none), True)

        def body(c, carry, hh=hh, sl=sl, step=step):
            r = pl.multiple_of(c * ck, ck)
            return step(k_ref[pl.ds(r, ck), sl], vt_ref[hh, c], carry, False)

        carry = lax.fori_loop(1, n_chunks, body, carry)
        if has_ctx:
            carry = step(kc_ref[:, sl], vct_ref[hh], carry, False)

        r = acc_ref[hh] / carry[1]
        o = (r[:, :tq] - lam * r[:, tq:]).T
        ms = jnp.mean(o * o, axis=-1, keepdims=True)
        y = o * lax.rsqrt(ms + NORM_EPS) * subln_ref[...]
        y = y * (1.0 - lam_init)
        o_ref[:, sl] = (y * _silu(g_ref[:, sl])).astype(BF16)


def _diff_attn(q, k, vt, ctx, proj3, lam_vecs, subln, lam_init):
    b, t, _ = q.shape
    n_chunks, ck = vt.shape[2], vt.shape[4]
    tq = _tile(t, DIFF_Q_TILE)
    hps = max(1, min(N_HEADS, DIFF_Q_TILE // t))
    hw = hps * LANES
    has_ctx = ctx is not None
    qspec = pl.BlockSpec((None, tq, hw), lambda bi, h, i: (bi, i, h))
    in_specs = [pl.BlockSpec((4, DIFF_QK_DIM), lambda bi, h, i: (0, 0)),
                pl.BlockSpec((1, LANES), lambda bi, h, i: (0, 0)),
                qspec,
                pl.BlockSpec((None, t, hw), lambda bi, h, i: (bi, 0, h)),
                pl.BlockSpec((None, hps, n_chunks, LANES, ck), lambda bi, h, i: (bi, h, 0, 0, 0))]
    args = [lam_vecs, subln, q, k, vt]
    if has_ctx:
        p = ctx[0].shape[1]
        in_specs += [pl.BlockSpec((None, p, hw), lambda bi, h, i: (bi, 0, h)),
                     pl.BlockSpec((None, hps, LANES, p), lambda bi, h, i: (bi, h, 0, 0))]
        args += list(ctx)
    in_specs.append(pl.BlockSpec((None, tq, hw), lambda bi, h, i: (bi, i, OFF_DG // hw + h)))
    args.append(proj3)
    return pl.pallas_call(
        functools.partial(_diff_attn_kernel, lam_init=lam_init, n_chunks=n_chunks, ck=ck, has_ctx=has_ctx),
        grid=(b, N_HEADS // hps, t // tq),
        in_specs=in_specs,
        out_specs=qspec,
        out_shape=jax.ShapeDtypeStruct((b, t, BRANCH_W), BF16),
        scratch_shapes=[pltpu.VMEM((hps, LANES, 2 * tq), F32)],
        compiler_params=_params("parallel", "parallel", "parallel"),
        name="diff_attn",
    )(*args)


def _win_attn_kernel(*refs, banded, n_blocks, blocks_per_step, has_ctx):
    it = iter(refs)
    sink_ref, q_ref, k_ref, vt_ref = (next(it) for _ in range(4))
    if has_ctx:
        kc_ref, vct_ref = next(it), next(it)
    g_ref, o_ref = next(it), next(it)

    kvh = pl.program_id(1)
    w = WINDOW
    gw = WIN_GROUP * w
    lane = lax.broadcasted_iota(jnp.int32, (1, gw), 1)
    sink = jnp.where(lane < w, sink_ref[kvh * WIN_GROUP], sink_ref[kvh * WIN_GROUP + 1]) * LOG2E
    if banded:
        key = lax.broadcasted_iota(jnp.int32, (w, gw), 0)
        qry = lax.broadcasted_iota(jnp.int32, (w, gw), 1) & (w - 1)
    shared = []
    if not banded:
        shared += [(k_ref[j * w:(j + 1) * w, :], vt_ref[j], None) for j in range(n_blocks)]

    q2s = [jnp.concatenate([q_ref[u * w:(u + 1) * w, j * LANES:(j + 1) * LANES] for j in range(WIN_GROUP)],
                           axis=0) for u in range(blocks_per_step)]
    if has_ctx:
        s_ctx = _dot_nt(kc_ref[...], jnp.concatenate(q2s, axis=0))
    stats, p_ctx = [], []

    for u in range(blocks_per_step):
        cols = slice(u * gw, (u + 1) * gw)
        q2 = q2s[u]
        segs = []
        if banded:
            n = pl.program_id(2) * blocks_per_step + u
            b_prev = jnp.maximum(n - 1, 0)
            b_next = jnp.minimum(n + 1, n_blocks - 1)
            keep_prev = (key >= qry) & (n > 0)
            keep_next = (key <= qry) & (n < n_blocks - 1)
            segs = [(k_ref[pl.ds(pl.multiple_of(b_prev * w, w), w), :], vt_ref[b_prev], keep_prev),
                    (k_ref[pl.ds(pl.multiple_of(n * w, w), w), :], vt_ref[n], None),
                    (k_ref[pl.ds(pl.multiple_of(b_next * w, w), w), :], vt_ref[b_next], keep_next)]
        segs += shared
        scores = []
        for kb, _, keep in segs:
            s = _dot_nt(kb, q2)
            scores.append(s if keep is None else jnp.where(keep, s, -jnp.inf))
        m = sink
        for s in scores:
            m = jnp.maximum(m, jnp.max(s, axis=0, keepdims=True))
        if has_ctx:
            m = jnp.maximum(m, jnp.max(s_ctx[:, cols], axis=0, keepdims=True))
        l = jnp.exp2(sink - m)
        acc = None
        for s, (_, vtb, _) in zip(scores, segs):
            p = jnp.exp2(s - m)
            l = l + jnp.sum(p, axis=0, keepdims=True)
            pv = _dot(vtb, p.astype(BF16))
            acc = pv if acc is None else acc + pv
        if has_ctx:
            p = jnp.exp2(s_ctx[:, cols] - m)
            l = l + jnp.sum(p, axis=0, keepdims=True)
            p_ctx.append(p.astype(BF16))
        stats.append((acc, l))

    if has_ctx:
        pv_ctx = _dot(vct_ref[...], jnp.concatenate(p_ctx, axis=1))
    for u in range(blocks_per_step):
        rows = slice(u * w, (u + 1) * w)
        acc, l = stats[u]
        if has_ctx:
            acc = acc + pv_ctx[:, u * gw:(u + 1) * gw]
        o = (acc / l).T
        for j in range(WIN_GROUP):
            sl = slice(j * LANES, (j + 1) * LANES)
            o_ref[rows, sl] = (o[j * w:(j + 1) * w] * _silu(g_ref[rows, sl])).astype(BF16)


def _win_attn(q, k, vt, ctx, proj3, sink, banded):
    b, t, _ = q.shape
    nb = t // WINDOW
    gw = WIN_GROUP * LANES
    has_ctx = ctx is not None
    per_step = _tile(nb, WIN_BLOCKS_PER_STEP)
    tq = per_step * WINDOW
    qspec = pl.BlockSpec((None, tq, gw), lambda bi, h, i: (bi, i, h))
    in_specs = [pl.BlockSpec(memory_space=pltpu.SMEM), qspec,
                pl.BlockSpec((None, t, LANES), lambda bi, h, i: (bi, 0, h)),
                pl.BlockSpec((None, None, nb, LANES, WINDOW), lambda bi, h, i: (bi, h, 0, 0, 0))]
    args = [sink, q, k, vt]
    if has_ctx:
        p = ctx[0].shape[1]
        in_specs += [pl.BlockSpec((None, p, LANES), lambda bi, h, i: (bi, 0, h)),
                     pl.BlockSpec((None, None, LANES, p), lambda bi, h, i: (bi, h, 0, 0))]
        args += list(ctx)
    in_specs.append(pl.BlockSpec((None, tq, gw), lambda bi, h, i: (bi, i, OFF_WG // gw + h)))
    args.append(proj3)
    return pl.pallas_call(
        functools.partial(_win_attn_kernel, banded=banded, n_blocks=nb, blocks_per_step=per_step,
                          has_ctx=has_ctx),
        grid=(b, WIN_KV_HEADS, nb // per_step),
        in_specs=in_specs,
        out_specs=qspec,
        out_shape=jax.ShapeDtypeStruct((b, t, BRANCH_W), BF16),
        compiler_params=_params("parallel", "parallel", "parallel"),
        name="win_attn",
    )(*args)


def _hgrn_tables(reverse):
    c = HG_CHUNK
    t = np.arange(c)[:, None]
    u = np.arange(c)[None, :]
    mats, masks = [], []
    for m in range(HG_LEVELS):
        half, blk = 1 << m, 2 << m
        r = t - t % blk + half - 1
        right = (t % blk) >= half
        if not reverse:
            e_q = (u >= r + 1) & (u <= t)
            e_k = (u >= t + 1) & (u <= r)
            mats.append(np.where(right, e_q, e_k))
            masks.append(right & ~right.T & (t // blk == u // blk))
        else:
            e_q = (u >= t) & (u <= r)
            e_k = (u >= r + 1) & (u <= t - 1)
            mats.append(np.where(right, e_k, e_q))
            masks.append(~right & right.T & (t // blk == u // blk))
    mats = mats[:HG_MXU_LEVELS] + [(u <= t) if not reverse else (u >= t)]
    masks.append(t == u)
    return (np.concatenate(mats, 0).astype(np.float32), np.stack(masks).astype(np.float32))


def _hgrn_constants():
    c = HG_CHUNK
    tabs = [_hgrn_tables(False), _hgrn_tables(True)]
    mats = np.stack([np.tile(tb[0], (1, 3)) for tb in tabs])
    masks = np.zeros((HG_LEVELS + 1, 4, c, LANES), np.float32)
    for a in range(4):
        lo = (a % 2) * c
        masks[:, a, :, lo:lo + c] = tabs[a // 2][1]
    return mats, masks


def _hgrn_kernel(lb_ref, s0_ref, mat_ref, mask_ref,
                 qf_ref, vf_ref, zf_ref, qb_ref, vb_ref, zb_ref, *rest, layer, n_chunks):
    of_ref, ob_ref, sout_ref, st_ref = rest[-4:]
    i = pl.program_id(2)
    c = HG_CHUNK
    pair = 2
    pw = pair * LANES

    @pl.when(i == 0)
    def _():
        for d in range(2):
            for hh in range(pair):
                st_ref[2 * d + hh] = s0_ref[d, hh]

    def lower_bound(d):
        rows = [lb_ref[d, r:r + 1, :] for r in range(lb_ref.shape[1])]
        mx = functools.reduce(jnp.maximum, rows)
        es = [jnp.exp(r - mx) for r in rows]
        den = functools.reduce(lambda a, b_: a + b_, es)
        ps = [e / den for e in es]
        return functools.reduce(lambda a, b_: a + b_, ps[:layer + 1]) - ps[0]

    bit_row = lax.broadcasted_iota(jnp.int32, (c, pw), 0)

    def side_rows(m, d, on_query_rows, on_key_rows):
        half = 1 << m
        if half % 8:
            return jnp.where(((bit_row >> m) & 1) == (1 - d), on_query_rows, on_key_rows)
        slabs = [(on_query_rows if (r0 // half) % 2 == 1 - d else on_key_rows)[r0:r0 + half]
                 for r0 in range(0, c, half)]
        return jnp.concatenate(slabs, axis=0)

    def decays(d, z, lb):
        e = jnp.exp(-jnp.abs(z))
        r = 1.0 / (1.0 + e)
        er = e * r
        pos = z >= 0.0
        f = lb + (1.0 - lb) * jnp.where(pos, r, er)
        kk = (1.0 - lb) * jnp.where(pos, er, r)
        logf = jnp.log(jnp.maximum(f, F_FLOOR)) * LOG2E
        parts = jnp.concatenate(_split_bf16(logf, 3), axis=0)
        e_small = _dot(mat_ref[d], parts)
        run = e_small[HG_MXU_LEVELS * c:]
        levels = [jnp.exp2(e_small[m * c:(m + 1) * c]) for m in range(HG_MXU_LEVELS)]
        for m in range(HG_MXU_LEVELS, HG_LEVELS):
            half, blk = 1 << m, 2 << m
            edge = half - 1 + d
            rho = jnp.concatenate(
                [jnp.broadcast_to(run[b0 + edge:b0 + edge + 1], (blk, pw)) for b0 in range(0, c, blk)], axis=0)
            diff = run - rho
            levels.append(jnp.exp2(side_rows(m, d, diff, -diff)))
        last = (c - 1) if d == 0 else 0
        total = run[last:last + 1]
        return kk, levels, jnp.exp2(run), jnp.exp2(total - run), jnp.exp2(total)

    lbs = (lower_bound(0), lower_bound(1))

    def body(ci, _):
        rows = (pl.ds(pl.multiple_of(ci * c, c), c), pl.ds(pl.multiple_of((n_chunks - 1 - ci) * c, c), c))
        qs = (qf_ref[rows[0], :], qb_ref[rows[1], :])
        vs = (vf_ref[rows[0], :].astype(BF16), vb_ref[rows[1], :].astype(BF16))
        zs = (zf_ref[rows[0], :], zb_ref[rows[1], :])
        dec = [decays(d, zs[d], lbs[d]) for d in range(2)]

        def stacked(fn):
            wide = [fn(d).astype(BF16) for d in range(2)]
            return jnp.concatenate([wide[a // 2][:, (a % 2) * LANES:(a % 2 + 1) * LANES] for a in range(4)],
                                   axis=0)

        def pair_weights(g, m):
            return [mask_ref[m, a] * g[a * c:(a + 1) * c, (a // 2) * LANES:(a // 2 + 1) * LANES]
                    for a in range(4)]

        acc = pair_weights(_dot_nt(stacked(lambda d: qs[d]), stacked(lambda d: dec[d][0])), HG_LEVELS)
        for m in range(HG_LEVELS):
            def level_rows(d, m=m):
                return side_rows(m, d, qs[d], dec[d][0]) * dec[d][1][m]
            lm = stacked(level_rows)
            acc = [x + y for x, y in zip(acc, pair_weights(_dot_nt(lm, lm), m))]

        outs = (of_ref, ob_ref)
        for a in range(4):
            d, hh = a // 2, a % 2
            sl = slice(hh * LANES, (hh + 1) * LANES)
            kk, _, from_start, to_end, whole = dec[d]
            v = vs[d][:, sl]
            qe = (qs[d][:, sl] * from_start[:, sl]).astype(BF16)
            ke = (kk[:, sl] * to_end[:, sl]).astype(BF16)
            st = st_ref[a]
            lhs = jnp.concatenate([acc[a].astype(BF16), qe], axis=1)
            rhs = jnp.concatenate([v, v, st.astype(BF16)], axis=0)
            outs[d][rows[d], sl] = _dot(lhs, rhs)
            decay_col = jnp.broadcast_to(whole[:, sl], (LANES, LANES)).T
            st_ref[a] = st * decay_col + lax.dot_general(ke, v, TN_DIMS, preferred_element_type=F32)
        return 0

    lax.fori_loop(0, n_chunks, body, 0, unroll=4)

    @pl.when(i == pl.num_programs(2) - 1)
    def _():
        for d in range(2):
            for hh in range(pair):
                sout_ref[d, hh] = st_ref[2 * d + hh]


def _hgrn(proj3, hg_lb, state0, layer, states_out=None):
    b, t, _ = proj3.shape
    tb = _tile(t, 512)
    nt = t // tb
    pw = 2 * LANES
    mats, masks = _hgrn_constants()
    mats = jnp.asarray(mats, BF16)
    masks = jnp.asarray(masks, F32)

    def fwd(off):
        return pl.BlockSpec((None, tb, pw), lambda bi, hp, i: (bi, i, off // pw + hp))

    def bwd(off):
        return pl.BlockSpec((None, tb, pw), lambda bi, hp, i: (bi, nt - 1 - i, off // pw + hp))

    st_spec = pl.BlockSpec((None, 2, 2, LANES, LANES), lambda bi, hp, i: (bi, 0, hp, 0, 0))
    in_specs = [
        pl.BlockSpec((2, hg_lb.shape[1], pw), lambda bi, hp, i: (0, 0, hp)),
        st_spec,
        pl.BlockSpec(mats.shape, lambda bi, hp, i: (0, 0, 0)),
        pl.BlockSpec(masks.shape, lambda bi, hp, i: (0, 0, 0, 0)),
        fwd(OFF_HQ), fwd(OFF_HI), fwd(OFF_HFF),
        bwd(OFF_HQ), bwd(OFF_HI), bwd(OFF_HFB),
    ]
    args = [hg_lb, state0, mats, masks, proj3, proj3, proj3, proj3, proj3, proj3]
    aliases = {}
    if states_out is None:
        st_out_spec, st_out_shape = st_spec, jax.ShapeDtypeStruct(state0.shape, F32)
    else:
        aliases[len(args)] = 2
        in_specs.append(pl.BlockSpec(memory_space=pl.ANY))
        args.append(states_out)
        st_out_spec = pl.BlockSpec((None, None, 2, 2, LANES, LANES), lambda bi, hp, i: (bi, layer, 0, hp, 0, 0))
        st_out_shape = jax.ShapeDtypeStruct(states_out.shape, F32)
    return pl.pallas_call(
        functools.partial(_hgrn_kernel, layer=layer, n_chunks=tb // HG_CHUNK),
        grid=(b, N_HEADS // 2, nt),
        in_specs=in_specs,
        out_specs=[
            pl.BlockSpec((None, tb, pw), lambda bi, hp, i: (bi, i, hp)),
            pl.BlockSpec((None, tb, pw), lambda bi, hp, i: (bi, nt - 1 - i, hp)),
            st_out_spec,
        ],
        out_shape=[
            jax.ShapeDtypeStruct((b, t, BRANCH_W), F32),
            jax.ShapeDtypeStruct((b, t, BRANCH_W), F32),
            st_out_shape,
        ],
        scratch_shapes=[pltpu.VMEM((4, LANES, LANES), F32)],
        input_output_aliases=aliases,
        compiler_params=_params("parallel", "parallel", "arbitrary"),
        name="hgrn_scan",
    )(*args)


def _merge_kernel(yd_ref, of_ref, ob_ref, yw_ref, hg_ref, gain_ref, mg0_ref, mg1_ref, mg2_ref, w_ref,
                  o_ref, yh_ref):
    @pl.when(pl.program_id(1) == 0)
    def _():
        for j in range(N_HEADS):
            sl = slice(j * LANES, (j + 1) * LANES)
            y = _head_norm(of_ref[:, sl] + ob_ref[:, sl], gain_ref[...], LANES)
            yh_ref[:, sl] = (y * _silu(hg_ref[:, sl])).astype(BF16)

    acc = mg0_ref[...].astype(F32) * _dot(yd_ref[...], w_ref[0])
    acc = acc + mg1_ref[...].astype(F32) * _dot(yh_ref[...], w_ref[1])
    acc = acc + mg2_ref[...].astype(F32) * _dot(yw_ref[...], w_ref[2])
    o_ref[...] = acc.astype(BF16)


def _merge(yd, o_f, o_b, yw, proj, gates, hg_gain, w_branch, layer):
    m = yd.shape[0]
    d = w_branch.shape[-1]
    tm = _tile(m, 1024)
    tn = _tile(d, 256)
    row = lambda width: pl.BlockSpec((tm, width), lambda i, j: (i, 0))

    def mg(nb):
        return pl.BlockSpec((tm, tn), lambda i, j: (i, nb * d // tn + j))

    return pl.pallas_call(
        _merge_kernel,
        grid=(m // tm, d // tn),
        in_specs=[row(BRANCH_W), row(BRANCH_W), row(BRANCH_W), row(BRANCH_W),
                  pl.BlockSpec((tm, BRANCH_W), lambda i, j: (i, OFF_HG // BRANCH_W)),
                  pl.BlockSpec((1, LANES), lambda i, j: (0, 0)),
                  mg(0), mg(1), mg(2),
                  pl.BlockSpec((None, 3, BRANCH_W, tn), lambda i, j: (layer, 0, 0, j))],
        out_specs=pl.BlockSpec((tm, tn), lambda i, j: (i, j)),
        out_shape=jax.ShapeDtypeStruct((m, d), BF16),
        scratch_shapes=[pltpu.VMEM((tm, BRANCH_W), BF16)],
        compiler_params=_params("parallel", "arbitrary"),
        name="branch_merge",
    )(yd, o_f, o_b, yw, proj, hg_gain, gates, gates, gates, w_branch)


def _out_proj_kernel(x_ref, gate_ref, m_ref, w_ref, o_ref):
    o_ref[...] = x_ref[...] + gate_ref[...] * _dot(m_ref[...], w_ref[...])


def _out_proj(x2d, gate, merged, w_out, layer, rows_per_cond):
    m, d = x2d.shape
    tm = _tile(rows_per_cond, 1024)
    tn = _tile(d, 512)
    return pl.pallas_call(
        _out_proj_kernel,
        grid=(m // tm, d // tn),
        in_specs=[
            pl.BlockSpec((tm, tn), lambda i, j: (i, j)),
            pl.BlockSpec((None, 1, tn), lambda i, j: (i * tm // rows_per_cond, 0, j)),
            pl.BlockSpec((tm, d), lambda i, j: (i, 0)),
            pl.BlockSpec((None, d, tn), lambda i, j: (layer, 0, j)),
        ],
        out_specs=pl.BlockSpec((tm, tn), lambda i, j: (i, j)),
        out_shape=jax.ShapeDtypeStruct((m, d), F32),
        compiler_params=_params("parallel", "parallel"),
        name="out_proj",
    )(x2d, gate, merged, w_out)


def _rope_tables(t, d):
    rows = t // GRID_W
    row = jnp.repeat(jnp.arange(rows, dtype=F32), GRID_W)
    col = jnp.tile(jnp.arange(GRID_W, dtype=F32), rows)
    quarter = d // 4
    inv = ROPE_BASE ** (-jnp.arange(quarter, dtype=F32) / quarter)
    ar = row[:, None] * inv
    ac = col[:, None] * inv
    ang = jnp.concatenate([ar, ar, ac, ac], axis=-1)
    return tuple(jnp.tile(a, (1, PREP_W // d)) for a in (jnp.cos(ang), jnp.sin(ang)))


def _tile_gain(g, width=LANES):
    return jnp.tile(g, width // g.shape[0]).reshape(1, width)


def _layer(x, mod, l, w, rope_tabs, caches, state0, new_caches):
    is_context = new_caches is not None
    b, t, d = x.shape
    m = b * t
    rows_per_cond = m // mod.shape[0]
    x2d = x.reshape(m, d)
    shift, scale, gate = (mod[:, None, j * d:(j + 1) * d] for j in range(3))
    proj, merge_gates = _in_proj(x2d, shift, scale, w["norm_gain"][l].reshape(1, d), w["w_in"], l,
                                 rows_per_cond)
    gains = tuple(_tile_gain(w[name][l], PREP_W)
                  for name in ("diff_q_norm", "diff_k_norm", "win_q_norm", "win_k_norm"))
    prepped = _prep(proj, gains, rope_tabs, t, new_caches[:4] if is_context else None, l)
    dq, dk, wq, wk = (prepped[j].reshape(b, t, -1) for j in (0, 1, 3, 4))
    dv, wv = prepped[2], prepped[5]
    proj3 = proj.reshape(b, t, -1)
    lam_init = 0.8 - 0.6 * math.exp(-0.3 * l)
    if is_context:
        dctx = wctx = None
    else:
        ck_d, cv_d, ck_w, cv_w = caches
        dctx = (ck_d.reshape(b, -1, BRANCH_W).astype(BF16),
                jnp.transpose(cv_d, (0, 2, 3, 1)).astype(BF16))
        wctx = (ck_w.reshape(b, -1, WIN_KV_HEADS * LANES).astype(BF16),
                jnp.transpose(cv_w, (0, 2, 3, 1)).astype(BF16))
    yd = _diff_attn(dq, dk, dv, dctx, proj3, w["diff_lambda"][l], _tile_gain(w["diff_subln"][l]), lam_init)
    o_f, o_b, s_out = _hgrn(proj3, w["hg_lb"], state0, l, new_caches[4] if is_context else None)
    yw = _win_attn(wq, wk, wv, wctx, proj3, w["win_sink"][l], banded=not is_context)
    merged = _merge(yd.reshape(m, -1), o_f.reshape(m, -1), o_b.reshape(m, -1), yw.reshape(m, -1), proj,
                    merge_gates, _tile_gain(w["hg_out_norm"][l]), w["w_branch"], l)
    y = _out_proj(x2d, gate, merged, w["w_out"], l, rows_per_cond).reshape(b, t, d)
    if not is_context:
        return y, None
    return y, tuple(prepped[6:10]) + (s_out,)


def kernel(x_prompt, x_sample, cache_diff_k, cache_diff_v, cache_win_k, cache_win_v, state_hgrn, c, c_ctx, norm_gain, w_ada, b_ada, w_in, diff_q_norm, diff_k_norm, diff_lambda, diff_subln, hg_lb, hg_out_norm, win_q_norm, win_k_norm, win_sink, w_branch, w_out):
    depth = w_in.shape[0]
    dec_b, dec_t, d = x_sample.shape
    w = {"norm_gain": norm_gain, "w_in": w_in.astype(BF16), "diff_q_norm": diff_q_norm,
         "diff_k_norm": diff_k_norm, "diff_lambda": diff_lambda, "diff_subln": diff_subln,
         "hg_lb": hg_lb, "hg_out_norm": hg_out_norm, "win_q_norm": win_q_norm, "win_k_norm": win_k_norm,
         "win_sink": win_sink, "w_branch": w_branch.astype(BF16), "w_out": w_out.astype(BF16)}

    cond_rows = 8 * ((1 + dec_b + 7) // 8)
    cond = jnp.zeros((cond_rows, d), F32).at[0].set(c_ctx).at[1:1 + dec_b].set(c)
    mod = _ada(cond, w_ada, b_ada)

    y_prompt = x_prompt
    pb, pt = x_prompt.shape[:2]
    zero_state = jnp.zeros((pb,) + state_hgrn.shape[2:], F32)
    new = tuple(jnp.zeros((pb, depth, pt, width), F32)
                for width in (BRANCH_W, BRANCH_W, WIN_KV_HEADS * LANES, WIN_KV_HEADS * LANES))
    new += (jnp.zeros((pb, depth) + state_hgrn.shape[2:], F32),)
    for l in range(depth):
        y_prompt, new = _layer(y_prompt, mod[l, 0:1], l, w, None, None, zero_state, new)

    ropes = _rope_tables(dec_t, DIFF_QK_DIM) + _rope_tables(dec_t, LANES)
    y_sample = x_sample
    for l in range(depth):
        caches = (cache_diff_k[:, l], cache_diff_v[:, l], cache_win_k[:, l], cache_win_v[:, l])
        y_sample, _ = _layer(y_sample, mod[l, 1:1 + dec_b], l, w, ropes, caches, state_hgrn[:, l], None)

    return (y_prompt, y_sample,
            new[0].reshape(pb, depth, pt, N_HEADS, 2, DIFF_QK_DIM),
            new[1].reshape(pb, depth, pt, N_HEADS, LANES),
            new[2].reshape(pb, depth, pt, WIN_KV_HEADS, LANES),
            new[3].reshape(pb, depth, pt, WIN_KV_HEADS, LANES),
            new[4])
```

```python
import functools
import math

import numpy as np
import jax
import jax.numpy as jnp
from jax import lax
from jax.experimental import pallas as pl
from jax.experimental.pallas import tpu as pltpu

F32 = jnp.float32
BF16 = jnp.bfloat16

NORM_EPS = 1e-6
F_FLOOR = 1e-30
ROPE_BASE = 10000.0
GRID_W = 64
WINDOW = 128

LANES = 128
N_HEADS = 8
WIN_KV_HEADS = 4
WIN_GROUP = 2
DIFF_QK_DIM = 64
BRANCH_W = N_HEADS * LANES
HG_CHUNK = 64
HG_LEVELS = 6
HG_MXU_LEVELS = 3
DIFF_KV_CHUNK = 512
WIN_BLOCKS_PER_STEP = 8
DIFF_Q_TILE = 2048
LOG2E = math.log2(math.e)

OFF_DQ, OFF_DK, OFF_DV, OFF_DG = 0, 1024, 2048, 3072
OFF_HQ, OFF_HFF, OFF_HFB, OFF_HI, OFF_HG = 4096, 5120, 6144, 7168, 8192
OFF_WQ, OFF_WK, OFF_WV, OFF_WG = 9216, 10240, 10752, 11264
OFF_MG = 12288

VMEM_LIMIT = 48 * 1024 * 1024
VMEM_LIMIT_WIDE = 58 * 1024 * 1024

NT_DIMS = (((1,), (1,)), ((), ()))
TN_DIMS = (((0,), (0,)), ((), ()))


def _params(*sem, vmem=VMEM_LIMIT):
    return pltpu.CompilerParams(dimension_semantics=sem, vmem_limit_bytes=vmem)


def _tile(n, preferred):
    t = min(n, preferred)
    while n % t:
        t //= 2
    return t


def _dot(a, b):
    return jnp.dot(a, b, preferred_element_type=F32)


def _dot_nt(a, b):
    return lax.dot_general(a, b, NT_DIMS, preferred_element_type=F32)


def _sigmoid(x):
    return 0.5 * jnp.tanh(0.5 * x) + 0.5


def _silu(x):
    return x * _sigmoid(x)


def _split_bf16(x, terms):
    out = []
    for _ in range(terms - 1):
        hi = x.astype(BF16)
        out.append(hi)
        x = x - hi.astype(F32)
    out.append(x.astype(BF16))
    return out


def _ada_kernel(c_ref, w_ref, b_ref, o_ref):
    a = _silu(c_ref[...])
    a_hi, a_lo = _split_bf16(a, 2)
    w_hi, w_lo = _split_bf16(w_ref[...], 2)
    acc = _dot(a_hi, w_hi) + _dot(a_lo, w_hi) + _dot(a_hi, w_lo)
    o_ref[...] = acc + b_ref[...]


def _ada(cond, w_ada, b_ada):
    depth, d, n = w_ada.shape
    tn = _tile(n, 512)
    rows = cond.shape[0]
    return pl.pallas_call(
        _ada_kernel,
        grid=(depth, n // tn),
        in_specs=[
            pl.BlockSpec((rows, d), lambda l, j: (0, 0)),
            pl.BlockSpec((None, d, tn), lambda l, j: (l, 0, j)),
            pl.BlockSpec((None, 1, tn), lambda l, j: (l, 0, j)),
        ],
        out_specs=pl.BlockSpec((None, rows, tn), lambda l, j: (l, 0, j)),
        out_shape=jax.ShapeDtypeStruct((depth, rows, n), F32),
        compiler_params=_params("parallel", "parallel"),
        name="ada_mod",
    )(cond, w_ada, b_ada.reshape(depth, 1, n))


def _in_proj_kernel(x_ref, shift_ref, scale_ref, gain_ref, w_ref, o_ref, h_ref):
    @pl.when(pl.program_id(1) == 0)
    def _():
        x = x_ref[...]
        ms = jnp.mean(x * x, axis=-1, keepdims=True)
        y = x * lax.rsqrt(ms + NORM_EPS) * gain_ref[...]
        h_ref[...] = (y * (1.0 + scale_ref[...]) + shift_ref[...]).astype(BF16)

    o_ref[...] = _dot(h_ref[...], w_ref[...])


def _in_proj(x2d, shift, scale, gain, w_in, layer, rows_per_cond):
    m, d = x2d.shape
    n = w_in.shape[-1]
    tm = _tile(rows_per_cond, 1024)
    tn = _tile(n, 1024)
    return pl.pallas_call(
        _in_proj_kernel,
        grid=(m // tm, n // tn),
        in_specs=[
            pl.BlockSpec((tm, d), lambda i, j: (i, 0)),
            pl.BlockSpec((None, 1, d), lambda i, j: (i * tm // rows_per_cond, 0, 0)),
            pl.BlockSpec((None, 1, d), lambda i, j: (i * tm // rows_per_cond, 0, 0)),
            pl.BlockSpec((1, d), lambda i, j: (0, 0)),
            pl.BlockSpec((None, d, tn), lambda i, j: (layer, 0, j)),
        ],
        out_specs=pl.BlockSpec((tm, tn), lambda i, j: (i, j)),
        out_shape=jax.ShapeDtypeStruct((m, n), F32),
        scratch_shapes=[pltpu.VMEM((tm, d), BF16)],
        compiler_params=_params("parallel", "arbitrary"),
        name="in_proj",
    )(x2d, shift, scale, gain, w_in)


def _head_norm(x, gain, group):
    x2 = x * x
    if group == LANES:
        ms = jnp.sum(x2, axis=-1, keepdims=True) * (1.0 / LANES)
    else:
        lo = lax.broadcasted_iota(jnp.int32, x.shape, 1) < group
        s_lo = jnp.sum(jnp.where(lo, x2, 0.0), axis=-1, keepdims=True)
        s_hi = jnp.sum(jnp.where(lo, 0.0, x2), axis=-1, keepdims=True)
        ms = jnp.where(lo, s_lo, s_hi) * (1.0 / group)
    return x * lax.rsqrt(ms + NORM_EPS) * gain


PREP_W = 2 * LANES


def _group_mean_matrix(group):
    i = np.arange(PREP_W)
    return ((i[:, None] // group) == (i[None, :] // group)).astype(np.float32) / group


def _rotate_half_matrix(d):
    quarter = d // 4
    i = np.arange(PREP_W)
    first = (i % (2 * quarter)) < quarter
    mat = np.zeros((PREP_W, PREP_W), np.float32)
    mat[i[first] + quarter, i[first]] = -1.0
    mat[i[~first] - quarter, i[~first]] = 1.0
    return mat


def _prep_kernel(*refs, rope, keep_f32):
    it = iter(refs)
    dq_ref, dk_ref, dv_ref, wq_ref, wk_ref, wv_ref = (next(it) for _ in range(6))
    gdq_ref, gdk_ref, gwq_ref, gwk_ref, md_ref, mw_ref = (next(it) for _ in range(6))
    if rope:
        rd_ref, rw_ref, cd_ref, sd_ref, cw_ref, sw_ref = (next(it) for _ in range(6))
    if keep_f32:
        for _ in range(4):
            next(it)
    odq, odk, odv, owq, owk, owv = (next(it) for _ in range(6))
    if keep_f32:
        odk32, odv32, owk32, owv32 = (next(it) for _ in range(4))

    def pair(src, j, gain_ref, mean_ref, rot_ref, cos_ref, sin_ref):
        x = src[:, j * PREP_W:(j + 1) * PREP_W]
        hi, lo = _split_bf16(x * x, 2)
        ms = _dot(hi, mean_ref[...]) + _dot(lo, mean_ref[...])
        y = x * lax.rsqrt(ms + NORM_EPS) * gain_ref[...]
        if rope:
            y = y * cos_ref[...] + _dot(y.astype(BF16), rot_ref[...]) * sin_ref[...]
        return y

    dtabs = (rd_ref, cd_ref, sd_ref) if rope else (None, None, None)
    wtabs = (rw_ref, cw_ref, sw_ref) if rope else (None, None, None)
    for j in range(N_HEADS // 2):
        sl = slice(j * PREP_W, (j + 1) * PREP_W)
        q = pair(dq_ref, j, gdq_ref, md_ref, *dtabs)
        odq[:, sl] = (q * (DIFF_QK_DIM ** -0.5 * LOG2E)).astype(BF16)
        k = pair(dk_ref, j, gdk_ref, md_ref, *dtabs)
        odk[:, sl] = k.astype(BF16)
        if keep_f32:
            odk32[:, sl] = k
        q = pair(wq_ref, j, gwq_ref, mw_ref, *wtabs)
        owq[:, sl] = (q * (LANES ** -0.5 * LOG2E)).astype(BF16)
    for j in range(WIN_KV_HEADS // 2):
        sl = slice(j * PREP_W, (j + 1) * PREP_W)
        k = pair(wk_ref, j, gwk_ref, mw_ref, *wtabs)
        owk[:, sl] = k.astype(BF16)
        if keep_f32:
            owk32[:, sl] = k
    for j in range(N_HEADS):
        odv[j] = dv_ref[:, j * LANES:(j + 1) * LANES].T.astype(BF16)
    for j in range(WIN_KV_HEADS):
        for u in range(owv.shape[1]):
            owv[j, u] = wv_ref[u * WINDOW:(u + 1) * WINDOW, j * LANES:(j + 1) * LANES].T.astype(BF16)
    if keep_f32:
        odv32[...] = dv_ref[...]
        owv32[...] = wv_ref[...]


def _prep(proj, gains, rope_tabs, seq_len, caches=None, layer=0):
    m = proj.shape[0]
    keep_f32 = caches is not None
    tr = _tile(seq_len, DIFF_KV_CHUNK)
    nt = seq_len // tr
    w8, w4 = BRANCH_W, WIN_KV_HEADS * LANES
    rope = rope_tabs is not None

    def col(width, off):
        return pl.BlockSpec((tr, width), lambda i: (i, off // width))

    small = pl.BlockSpec((1, PREP_W), lambda i: (0, 0))
    square = pl.BlockSpec((PREP_W, PREP_W), lambda i: (0, 0))
    in_specs = [col(w8, OFF_DQ), col(w8, OFF_DK), col(w8, OFF_DV), col(w8, OFF_WQ),
                col(w4, OFF_WK), col(w4, OFF_WV), small, small, small, small, square, square]
    args = [proj] * 6 + list(gains) + [jnp.asarray(_group_mean_matrix(DIFF_QK_DIM), BF16),
                                        jnp.asarray(_group_mean_matrix(LANES), BF16)]
    if rope:
        tab = pl.BlockSpec((tr, PREP_W), lambda i: (i % nt, 0))
        in_specs += [square, square] + [tab] * 4
        args += [jnp.asarray(_rotate_half_matrix(DIFF_QK_DIM), BF16),
                 jnp.asarray(_rotate_half_matrix(LANES), BF16)] + list(rope_tabs)
    out8 = pl.BlockSpec((tr, w8), lambda i: (i, 0))
    out4 = pl.BlockSpec((tr, w4), lambda i: (i, 0))
    out_vt = pl.BlockSpec((None, N_HEADS, None, LANES, tr), lambda i: (i // nt, 0, i % nt, 0, 0))
    kb = tr // WINDOW
    out_wvt = pl.BlockSpec((None, WIN_KV_HEADS, kb, LANES, WINDOW), lambda i: (i // nt, 0, i % nt, 0, 0))
    out_specs = [out8, out8, out_vt, out8, out4, out_wvt]
    out_shape = [jax.ShapeDtypeStruct((m, w8), BF16), jax.ShapeDtypeStruct((m, w8), BF16),
                 jax.ShapeDtypeStruct((m // seq_len, N_HEADS, nt, LANES, tr), BF16),
                 jax.ShapeDtypeStruct((m, w8), BF16), jax.ShapeDtypeStruct((m, w4), BF16),
                 jax.ShapeDtypeStruct((m // seq_len, WIN_KV_HEADS, seq_len // WINDOW, LANES, WINDOW), BF16)]
    aliases = {}
    if keep_f32:
        for j, cache in enumerate(caches):
            aliases[len(args)] = len(out_shape)
            in_specs.append(pl.BlockSpec(memory_space=pl.ANY))
            args.append(cache)
            out_specs.append(pl.BlockSpec((None, None, tr, cache.shape[-1]),
                                          lambda i: (i // nt, layer, i % nt, 0)))
            out_shape.append(jax.ShapeDtypeStruct(cache.shape, F32))
    return pl.pallas_call(
        functools.partial(_prep_kernel, rope=rope, keep_f32=keep_f32),
        grid=(m // tr,),
        in_specs=in_specs,
        out_specs=out_specs,
        out_shape=out_shape,
        input_output_aliases=aliases,
        compiler_params=_params("parallel"),
        name="qk_prep",
    )(*args)


def _diff_attn_kernel(*refs, lam_init, n_chunks, ck, has_ctx):
    it = iter(refs)
    lam_ref, subln_ref, q_ref, k_ref, vt_ref = (next(it) for _ in range(5))
    if has_ctx:
        kc_ref, vct_ref = next(it), next(it)
    g_ref, o_ref, acc_ref = next(it), next(it), next(it)

    tq = q_ref.shape[0]
    lv = lam_ref[...]
    lam = (jnp.exp(jnp.sum(lv[0:1] * lv[1:2], axis=-1, keepdims=True))
           - jnp.exp(jnp.sum(lv[2:3] * lv[3:4], axis=-1, keepdims=True)) + lam_init)
    first = lax.broadcasted_iota(jnp.int32, (tq, LANES), 1) < DIFF_QK_DIM

    for hh in range(vt_ref.shape[0]):
        sl = slice(hh * LANES, (hh + 1) * LANES)
        q = q_ref[:, sl]
        zero = jnp.zeros_like(q)
        q2 = jnp.concatenate([jnp.where(first, q, zero), jnp.where(first, zero, q)], axis=0)

        def step(kb, vtb, carry, is_first, hh=hh, q2=q2):
            m, l = carry
            s = _dot_nt(kb, q2)
            m_new = jnp.maximum(m, jnp.max(s, axis=0, keepdims=True))
            p = jnp.exp2(s - m_new)
            pv = _dot(vtb, p.astype(BF16))
            if is_first:
                l = jnp.sum(p, axis=0, keepdims=True)
                acc_ref[hh] = pv
            else:
                alpha = jnp.exp2(m - m_new)
                l = alpha * l + jnp.sum(p, axis=0, keepdims=True)
                acc_ref[hh] = alpha * acc_ref[hh] + pv
            return m_new, l

        none = jnp.zeros((1, 2 * tq), F32)
        carry = step(k_ref[pl.ds(0, ck), sl], vt_ref[hh, 0], (none, none), True)

        def body(c, carry, hh=hh, sl=sl, step=step):
            r = pl.multiple_of(c * ck, ck)
            return step(k_ref[pl.ds(r, ck), sl], vt_ref[hh, c], carry, False)

        carry = lax.fori_loop(1, n_chunks, body, carry)
        if has_ctx:
            carry = step(kc_ref[:, sl], vct_ref[hh], carry, False)

        r = acc_ref[hh] / carry[1]
        o = (r[:, :tq] - lam * r[:, tq:]).T
        ms = jnp.mean(o * o, axis=-1, keepdims=True)
        y = o * lax.rsqrt(ms + NORM_EPS) * subln_ref[...]
        y = y * (1.0 - lam_init)
        o_ref[:, sl] = (y * _silu(g_ref[:, sl])).astype(BF16)


def _diff_attn(q, k, vt, ctx, proj3, lam_vecs, subln, lam_init):
    b, t, _ = q.shape
    n_chunks, ck = vt.shape[2], vt.shape[4]
    tq = _tile(t, DIFF_Q_TILE)
    hps = max(1, min(N_HEADS, DIFF_Q_TILE // t))
    hw = hps * LANES
    has_ctx = ctx is not None
    qspec = pl.BlockSpec((None, tq, hw), lambda bi, h, i: (bi, i, h))
    in_specs = [pl.BlockSpec((4, DIFF_QK_DIM), lambda bi, h, i: (0, 0)),
                pl.BlockSpec((1, LANES), lambda bi, h, i: (0, 0)),
                qspec,
                pl.BlockSpec((None, t, hw), lambda bi, h, i: (bi, 0, h)),
                pl.BlockSpec((None, hps, n_chunks, LANES, ck), lambda bi, h, i: (bi, h, 0, 0, 0))]
    args = [lam_vecs, subln, q, k, vt]
    if has_ctx:
        p = ctx[0].shape[1]
        in_specs += [pl.BlockSpec((None, p, hw), lambda bi, h, i: (bi, 0, h)),
                     pl.BlockSpec((None, hps, LANES, p), lambda bi, h, i: (bi, h, 0, 0))]
        args += list(ctx)
    in_specs.append(pl.BlockSpec((None, tq, hw), lambda bi, h, i: (bi, i, OFF_DG // hw + h)))
    args.append(proj3)
    return pl.pallas_call(
        functools.partial(_diff_attn_kernel, lam_init=lam_init, n_chunks=n_chunks, ck=ck, has_ctx=has_ctx),
        grid=(b, N_HEADS // hps, t // tq),
        in_specs=in_specs,
        out_specs=qspec,
        out_shape=jax.ShapeDtypeStruct((b, t, BRANCH_W), BF16),
        scratch_shapes=[pltpu.VMEM((hps, LANES, 2 * tq), F32)],
        compiler_params=_params("parallel", "parallel", "parallel"),
        name="diff_attn",
    )(*args)


def _win_attn_kernel(*refs, banded, n_blocks, blocks_per_step, has_ctx):
    it = iter(refs)
    sink_ref, q_ref, k_ref, vt_ref = (next(it) for _ in range(4))
    if has_ctx:
        kc_ref, vct_ref = next(it), next(it)
    g_ref, o_ref = next(it), next(it)

    kvh = pl.program_id(1)
    w = WINDOW
    gw = WIN_GROUP * w
    lane = lax.broadcasted_iota(jnp.int32, (1, gw), 1)
    sink = jnp.where(lane < w, sink_ref[kvh * WIN_GROUP], sink_ref[kvh * WIN_GROUP + 1]) * LOG2E
    if banded:
        key = lax.broadcasted_iota(jnp.int32, (w, gw), 0)
        qry = lax.broadcasted_iota(jnp.int32, (w, gw), 1) & (w - 1)
    shared = []
    if not banded:
        shared += [(k_ref[j * w:(j + 1) * w, :], vt_ref[j], None) for j in range(n_blocks)]

    q2s = [jnp.concatenate([q_ref[u * w:(u + 1) * w, j * LANES:(j + 1) * LANES] for j in range(WIN_GROUP)],
                           axis=0) for u in range(blocks_per_step)]
    if has_ctx:
        s_ctx = _dot_nt(kc_ref[...], jnp.concatenate(q2s, axis=0))
    stats, p_ctx = [], []

    for u in range(blocks_per_step):
        cols = slice(u * gw, (u + 1) * gw)
        q2 = q2s[u]
        segs = []
        if banded:
            n = pl.program_id(2) * blocks_per_step + u
            b_prev = jnp.maximum(n - 1, 0)
            b_next = jnp.minimum(n + 1, n_blocks - 1)
            keep_prev = (key >= qry) & (n > 0)
            keep_next = (key <= qry) & (n < n_blocks - 1)
            segs = [(k_ref[pl.ds(pl.multiple_of(b_prev * w, w), w), :], vt_ref[b_prev], keep_prev),
                    (k_ref[pl.ds(pl.multiple_of(n * w, w), w), :], vt_ref[n], None),
                    (k_ref[pl.ds(pl.multiple_of(b_next * w, w), w), :], vt_ref[b_next], keep_next)]
        segs += shared
        scores = []
        for kb, _, keep in segs:
            s = _dot_nt(kb, q2)
            scores.append(s if keep is None else jnp.where(keep, s, -jnp.inf))
        m = sink
        for s in scores:
            m = jnp.maximum(m, jnp.max(s, axis=0, keepdims=True))
        if has_ctx:
            m = jnp.maximum(m, jnp.max(s_ctx[:, cols], axis=0, keepdims=True))
        l = jnp.exp2(sink - m)
        acc = None
        for s, (_, vtb, _) in zip(scores, segs):
            p = jnp.exp2(s - m)
            l = l + jnp.sum(p, axis=0, keepdims=True)
            pv = _dot(vtb, p.astype(BF16))
            acc = pv if acc is None else acc + pv
        if has_ctx:
            p = jnp.exp2(s_ctx[:, cols] - m)
            l = l + jnp.sum(p, axis=0, keepdims=True)
            p_ctx.append(p.astype(BF16))
        stats.append((acc, l))

    if has_ctx:
        pv_ctx = _dot(vct_ref[...], jnp.concatenate(p_ctx, axis=1))
    for u in range(blocks_per_step):
        rows = slice(u * w, (u + 1) * w)
        acc, l = stats[u]
        if has_ctx:
            acc = acc + pv_ctx[:, u * gw:(u + 1) * gw]
        o = (acc / l).T
        for j in range(WIN_GROUP):
            sl = slice(j * LANES, (j + 1) * LANES)
            o_ref[rows, sl] = (o[j * w:(j + 1) * w] * _silu(g_ref[rows, sl])).astype(BF16)


def _win_attn(q, k, vt, ctx, proj3, sink, banded):
    b, t, _ = q.shape
    nb = t // WINDOW
    gw = WIN_GROUP * LANES
    has_ctx = ctx is not None
    per_step = _tile(nb, WIN_BLOCKS_PER_STEP)
    tq = per_step * WINDOW
    qspec = pl.BlockSpec((None, tq, gw), lambda bi, h, i: (bi, i, h))
    in_specs = [pl.BlockSpec(memory_space=pltpu.SMEM), qspec,
                pl.BlockSpec((None, t, LANES), lambda bi, h, i: (bi, 0, h)),
                pl.BlockSpec((None, None, nb, LANES, WINDOW), lambda bi, h, i: (bi, h, 0, 0, 0))]
    args = [sink, q, k, vt]
    if has_ctx:
        p = ctx[0].shape[1]
        in_specs += [pl.BlockSpec((None, p, LANES), lambda bi, h, i: (bi, 0, h)),
                     pl.BlockSpec((None, None, LANES, p), lambda bi, h, i: (bi, h, 0, 0))]
        args += list(ctx)
    in_specs.append(pl.BlockSpec((None, tq, gw), lambda bi, h, i: (bi, i, OFF_WG // gw + h)))
    args.append(proj3)
    return pl.pallas_call(
        functools.partial(_win_attn_kernel, banded=banded, n_blocks=nb, blocks_per_step=per_step,
                          has_ctx=has_ctx),
        grid=(b, WIN_KV_HEADS, nb // per_step),
        in_specs=in_specs,
        out_specs=qspec,
        out_shape=jax.ShapeDtypeStruct((b, t, BRANCH_W), BF16),
        compiler_params=_params("parallel", "parallel", "parallel"),
        name="win_attn",
    )(*args)


def _hgrn_tables(reverse):
    c = HG_CHUNK
    t = np.arange(c)[:, None]
    u = np.arange(c)[None, :]
    mats, masks = [], []
    for m in range(HG_LEVELS):
        half, blk = 1 << m, 2 << m
        r = t - t % blk + half - 1
        right = (t % blk) >= half
        if not reverse:
            e_q = (u >= r + 1) & (u <= t)
            e_k = (u >= t + 1) & (u <= r)
            mats.append(np.where(right, e_q, e_k))
            masks.append(right & ~right.T & (t // blk == u // blk))
        else:
            e_q = (u >= t) & (u <= r)
            e_k = (u >= r + 1) & (u <= t - 1)
            mats.append(np.where(right, e_k, e_q))
            masks.append(~right & right.T & (t // blk == u // blk))
    mats = mats[:HG_MXU_LEVELS] + [(u <= t) if not reverse else (u >= t)]
    masks.append(t == u)
    return (np.concatenate(mats, 0).astype(np.float32), np.stack(masks).astype(np.float32))


def _hgrn_constants():
    c = HG_CHUNK
    tabs = [_hgrn_tables(False), _hgrn_tables(True)]
    mats = np.stack([np.tile(tb[0], (1, 3)) for tb in tabs])
    masks = np.zeros((HG_LEVELS + 1, 4, c, LANES), np.float32)
    for a in range(4):
        lo = (a % 2) * c
        masks[:, a, :, lo:lo + c] = tabs[a // 2][1]
    return mats, masks


def _hgrn_kernel(lb_ref, s0_ref, mat_ref, mask_ref,
                 qf_ref, vf_ref, zf_ref, qb_ref, vb_ref, zb_ref, *rest, layer, n_chunks):
    of_ref, ob_ref, sout_ref, st_ref = rest[-4:]
    i = pl.program_id(2)
    c = HG_CHUNK
    pair = 2
    pw = pair * LANES

    @pl.when(i == 0)
    def _():
        for d in range(2):
            for hh in range(pair):
                st_ref[2 * d + hh] = s0_ref[d, hh]

    def lower_bound(d):
        rows = [lb_ref[d, r:r + 1, :] for r in range(lb_ref.shape[1])]
        mx = functools.reduce(jnp.maximum, rows)
        es = [jnp.exp(r - mx) for r in rows]
        den = functools.reduce(lambda a, b_: a + b_, es)
        ps = [e / den for e in es]
        return functools.reduce(lambda a, b_: a + b_, ps[:layer + 1]) - ps[0]

    bit_row = lax.broadcasted_iota(jnp.int32, (c, pw), 0)

    def side_rows(m, d, on_query_rows, on_key_rows):
        half = 1 << m
        if half % 8:
            return jnp.where(((bit_row >> m) & 1) == (1 - d), on_query_rows, on_key_rows)
        slabs = [(on_query_rows if (r0 // half) % 2 == 1 - d else on_key_rows)[r0:r0 + half]
                 for r0 in range(0, c, half)]
        return jnp.concatenate(slabs, axis=0)

    def decays(d, z, lb):
        e = jnp.exp(-jnp.abs(z))
        r = 1.0 / (1.0 + e)
        er = e * r
        pos = z >= 0.0
        f = lb + (1.0 - lb) * jnp.where(pos, r, er)
        kk = (1.0 - lb) * jnp.where(pos, er, r)
        logf = jnp.log(jnp.maximum(f, F_FLOOR)) * LOG2E
        parts = jnp.concatenate(_split_bf16(logf, 3), axis=0)
        e_small = _dot(mat_ref[d], parts)
        run = e_small[HG_MXU_LEVELS * c:]
        levels = [jnp.exp2(e_small[m * c:(m + 1) * c]) for m in range(HG_MXU_LEVELS)]
        for m in range(HG_MXU_LEVELS, HG_LEVELS):
            half, blk = 1 << m, 2 << m
            edge = half - 1 + d
            rho = jnp.concatenate(
                [jnp.broadcast_to(run[b0 + edge:b0 + edge + 1], (blk, pw)) for b0 in range(0, c, blk)], axis=0)
            diff = run - rho
            levels.append(jnp.exp2(side_rows(m, d, diff, -diff)))
        last = (c - 1) if d == 0 else 0
        total = run[last:last + 1]
        return kk, levels, jnp.exp2(run), jnp.exp2(total - run), jnp.exp2(total)

    lbs = (lower_bound(0), lower_bound(1))

    def body(ci, _):
        rows = (pl.ds(pl.multiple_of(ci * c, c), c), pl.ds(pl.multiple_of((n_chunks - 1 - ci) * c, c), c))
        qs = (qf_ref[rows[0], :], qb_ref[rows[1], :])
        vs = (vf_ref[rows[0], :].astype(BF16), vb_ref[rows[1], :].astype(BF16))
        zs = (zf_ref[rows[0], :], zb_ref[rows[1], :])
        dec = [decays(d, zs[d], lbs[d]) for d in range(2)]

        def stacked(fn):
            wide = [fn(d).astype(BF16) for d in range(2)]
            return jnp.concatenate([wide[a // 2][:, (a % 2) * LANES:(a % 2 + 1) * LANES] for a in range(4)],
                                   axis=0)

        def pair_weights(g, m):
            return [mask_ref[m, a] * g[a * c:(a + 1) * c, (a // 2) * LANES:(a // 2 + 1) * LANES]
                    for a in range(4)]

        acc = pair_weights(_dot_nt(stacked(lambda d: qs[d]), stacked(lambda d: dec[d][0])), HG_LEVELS)
        for m in range(HG_LEVELS):
            def level_rows(d, m=m):
                return side_rows(m, d, qs[d], dec[d][0]) * dec[d][1][m]
            lm = stacked(level_rows)
            acc = [x + y for x, y in zip(acc, pair_weights(_dot_nt(lm, lm), m))]

        outs = (of_ref, ob_ref)
        for a in range(4):
            d, hh = a // 2, a % 2
            sl = slice(hh * LANES, (hh + 1) * LANES)
            kk, _, from_start, to_end, whole = dec[d]
            v = vs[d][:, sl]
            qe = (qs[d][:, sl] * from_start[:, sl]).astype(BF16)
            ke = (kk[:, sl] * to_end[:, sl]).astype(BF16)
            st = st_ref[a]
            lhs = jnp.concatenate([acc[a].astype(BF16), qe], axis=1)
            rhs = jnp.concatenate([v, v, st.astype(BF16)], axis=0)
            outs[d][rows[d], sl] = _dot(lhs, rhs)
            decay_col = jnp.broadcast_to(whole[:, sl], (LANES, LANES)).T
            st_ref[a] = st * decay_col + lax.dot_general(ke, v, TN_DIMS, preferred_element_type=F32)
        return 0

    lax.fori_loop(0, n_chunks, body, 0, unroll=8)

    @pl.when(i == pl.num_programs(2) - 1)
    def _():
        for d in range(2):
            for hh in range(pair):
                sout_ref[d, hh] = st_ref[2 * d + hh]


def _hgrn(proj3, hg_lb, state0, layer, states_out=None):
    b, t, _ = proj3.shape
    tb = _tile(t, 512)
    nt = t // tb
    pw = 2 * LANES
    mats, masks = _hgrn_constants()
    mats = jnp.asarray(mats, BF16)
    masks = jnp.asarray(masks, F32)

    def fwd(off):
        return pl.BlockSpec((None, tb, pw), lambda bi, hp, i: (bi, i, off // pw + hp))

    def bwd(off):
        return pl.BlockSpec((None, tb, pw), lambda bi, hp, i: (bi, nt - 1 - i, off // pw + hp))

    st_spec = pl.BlockSpec((None, 2, 2, LANES, LANES), lambda bi, hp, i: (bi, 0, hp, 0, 0))
    in_specs = [
        pl.BlockSpec((2, hg_lb.shape[1], pw), lambda bi, hp, i: (0, 0, hp)),
        st_spec,
        pl.BlockSpec(mats.shape, lambda bi, hp, i: (0, 0, 0)),
        pl.BlockSpec(masks.shape, lambda bi, hp, i: (0, 0, 0, 0)),
        fwd(OFF_HQ), fwd(OFF_HI), fwd(OFF_HFF),
        bwd(OFF_HQ), bwd(OFF_HI), bwd(OFF_HFB),
    ]
    args = [hg_lb, state0, mats, masks, proj3, proj3, proj3, proj3, proj3, proj3]
    aliases = {}
    if states_out is None:
        st_out_spec, st_out_shape = st_spec, jax.ShapeDtypeStruct(state0.shape, F32)
    else:
        aliases[len(args)] = 2
        in_specs.append(pl.BlockSpec(memory_space=pl.ANY))
        args.append(states_out)
        st_out_spec = pl.BlockSpec((None, None, 2, 2, LANES, LANES), lambda bi, hp, i: (bi, layer, 0, hp, 0, 0))
        st_out_shape = jax.ShapeDtypeStruct(states_out.shape, F32)
    return pl.pallas_call(
        functools.partial(_hgrn_kernel, layer=layer, n_chunks=tb // HG_CHUNK),
        grid=(b, N_HEADS // 2, nt),
        in_specs=in_specs,
        out_specs=[
            pl.BlockSpec((None, tb, pw), lambda bi, hp, i: (bi, i, hp)),
            pl.BlockSpec((None, tb, pw), lambda bi, hp, i: (bi, nt - 1 - i, hp)),
            st_out_spec,
        ],
        out_shape=[
            jax.ShapeDtypeStruct((b, t, BRANCH_W), F32),
            jax.ShapeDtypeStruct((b, t, BRANCH_W), F32),
            st_out_shape,
        ],
        scratch_shapes=[pltpu.VMEM((4, LANES, LANES), F32)],
        input_output_aliases=aliases,
        compiler_params=_params("parallel", "parallel", "arbitrary"),
        name="hgrn_scan",
    )(*args)


def _merge_kernel(yd_ref, of_ref, ob_ref, yw_ref, hg_ref, gain_ref, mg0_ref, mg1_ref, mg2_ref, w_ref,
                  o_ref, yh_ref):
    @pl.when(pl.program_id(1) == 0)
    def _():
        for j in range(N_HEADS):
            sl = slice(j * LANES, (j + 1) * LANES)
            y = _head_norm(of_ref[:, sl] + ob_ref[:, sl], gain_ref[...], LANES)
            yh_ref[:, sl] = (y * _silu(hg_ref[:, sl])).astype(BF16)

    acc = _sigmoid(mg0_ref[...]) * _dot(yd_ref[...], w_ref[0])
    acc = acc + _sigmoid(mg1_ref[...]) * _dot(yh_ref[...], w_ref[1])
    acc = acc + _sigmoid(mg2_ref[...]) * _dot(yw_ref[...], w_ref[2])
    o_ref[...] = acc.astype(BF16)


def _merge(yd, o_f, o_b, yw, proj, hg_gain, w_branch, layer):
    m = yd.shape[0]
    d = w_branch.shape[-1]
    tm = _tile(m, 1024)
    tn = _tile(d, 512)
    row = lambda width: pl.BlockSpec((tm, width), lambda i, j: (i, 0))

    def mg(nb):
        return pl.BlockSpec((tm, tn), lambda i, j: (i, (OFF_MG + nb * d) // tn + j))

    return pl.pallas_call(
        _merge_kernel,
        grid=(m // tm, d // tn),
        in_specs=[row(BRANCH_W), row(BRANCH_W), row(BRANCH_W), row(BRANCH_W),
                  pl.BlockSpec((tm, BRANCH_W), lambda i, j: (i, OFF_HG // BRANCH_W)),
                  pl.BlockSpec((1, LANES), lambda i, j: (0, 0)),
                  mg(0), mg(1), mg(2),
                  pl.BlockSpec((None, 3, BRANCH_W, tn), lambda i, j: (layer, 0, 0, j))],
        out_specs=pl.BlockSpec((tm, tn), lambda i, j: (i, j)),
        out_shape=jax.ShapeDtypeStruct((m, d), BF16),
        scratch_shapes=[pltpu.VMEM((tm, BRANCH_W), BF16)],
        compiler_params=_params("parallel", "arbitrary", vmem=VMEM_LIMIT_WIDE),
        name="branch_merge",
    )(yd, o_f, o_b, yw, proj, hg_gain, proj, proj, proj, w_branch)


def _out_proj_kernel(x_ref, gate_ref, m_ref, w_ref, o_ref):
    o_ref[...] = x_ref[...] + gate_ref[...] * _dot(m_ref[...], w_ref[...])


def _out_proj(x2d, gate, merged, w_out, layer, rows_per_cond):
    m, d = x2d.shape
    tm = _tile(rows_per_cond, 2048)
    tn = _tile(d, 512)
    return pl.pallas_call(
        _out_proj_kernel,
        grid=(m // tm, d // tn),
        in_specs=[
            pl.BlockSpec((tm, tn), lambda i, j: (i, j)),
            pl.BlockSpec((None, 1, tn), lambda i, j: (i * tm // rows_per_cond, 0, j)),
            pl.BlockSpec((tm, d), lambda i, j: (i, 0)),
            pl.BlockSpec((None, d, tn), lambda i, j: (layer, 0, j)),
        ],
        out_specs=pl.BlockSpec((tm, tn), lambda i, j: (i, j)),
        out_shape=jax.ShapeDtypeStruct((m, d), F32),
        compiler_params=_params("parallel", "parallel"),
        name="out_proj",
    )(x2d, gate, merged, w_out)


def _rope_tables(t, d):
    rows = t // GRID_W
    row = jnp.repeat(jnp.arange(rows, dtype=F32), GRID_W)
    col = jnp.tile(jnp.arange(GRID_W, dtype=F32), rows)
    quarter = d // 4
    inv = ROPE_BASE ** (-jnp.arange(quarter, dtype=F32) / quarter)
    ar = row[:, None] * inv
    ac = col[:, None] * inv
    ang = jnp.concatenate([ar, ar, ac, ac], axis=-1)
    return tuple(jnp.tile(a, (1, PREP_W // d)) for a in (jnp.cos(ang), jnp.sin(ang)))


def _tile_gain(g, width=LANES):
    return jnp.tile(g, width // g.shape[0]).reshape(1, width)


def _layer(x, mod, l, w, rope_tabs, caches, state0, new_caches):
    is_context = new_caches is not None
    b, t, d = x.shape
    m = b * t
    rows_per_cond = m // mod.shape[0]
    x2d = x.reshape(m, d)
    shift, scale, gate = (mod[:, None, j * d:(j + 1) * d] for j in range(3))
    proj = _in_proj(x2d, shift, scale, w["norm_gain"][l].reshape(1, d), w["w_in"], l, rows_per_cond)
    gains = tuple(_tile_gain(w[name][l], PREP_W)
                  for name in ("diff_q_norm", "diff_k_norm", "win_q_norm", "win_k_norm"))
    prepped = _prep(proj, gains, rope_tabs, t, new_caches[:4] if is_context else None, l)
    dq, dk, wq, wk = (prepped[j].reshape(b, t, -1) for j in (0, 1, 3, 4))
    dv, wv = prepped[2], prepped[5]
    proj3 = proj.reshape(b, t, -1)
    lam_init = 0.8 - 0.6 * math.exp(-0.3 * l)
    if is_context:
        dctx = wctx = None
    else:
        ck_d, cv_d, ck_w, cv_w = caches
        dctx = (ck_d.reshape(b, -1, BRANCH_W).astype(BF16),
                jnp.transpose(cv_d, (0, 2, 3, 1)).astype(BF16))
        wctx = (ck_w.reshape(b, -1, WIN_KV_HEADS * LANES).astype(BF16),
                jnp.transpose(cv_w, (0, 2, 3, 1)).astype(BF16))
    yd = _diff_attn(dq, dk, dv, dctx, proj3, w["diff_lambda"][l], _tile_gain(w["diff_subln"][l]), lam_init)
    o_f, o_b, s_out = _hgrn(proj3, w["hg_lb"], state0, l, new_caches[4] if is_context else None)
    yw = _win_attn(wq, wk, wv, wctx, proj3, w["win_sink"][l], banded=not is_context)
    merged = _merge(yd.reshape(m, -1), o_f.reshape(m, -1), o_b.reshape(m, -1), yw.reshape(m, -1), proj,
                    _tile_gain(w["hg_out_norm"][l]), w["w_branch"], l)
    y = _out_proj(x2d, gate, merged, w["w_out"], l, rows_per_cond).reshape(b, t, d)
    if not is_context:
        return y, None
    return y, tuple(prepped[6:10]) + (s_out,)


def kernel(x_prompt, x_sample, cache_diff_k, cache_diff_v, cache_win_k, cache_win_v, state_hgrn, c, c_ctx, norm_gain, w_ada, b_ada, w_in, diff_q_norm, diff_k_norm, diff_lambda, diff_subln, hg_lb, hg_out_norm, win_q_norm, win_k_norm, win_sink, w_branch, w_out):
    depth = w_in.shape[0]
    dec_b, dec_t, d = x_sample.shape
    w = {"norm_gain": norm_gain, "w_in": w_in.astype(BF16), "diff_q_norm": diff_q_norm,
         "diff_k_norm": diff_k_norm, "diff_lambda": diff_lambda, "diff_subln": diff_subln,
         "hg_lb": hg_lb, "hg_out_norm": hg_out_norm, "win_q_norm": win_q_norm, "win_k_norm": win_k_norm,
         "win_sink": win_sink, "w_branch": w_branch.astype(BF16), "w_out": w_out.astype(BF16)}

    cond_rows = 8 * ((1 + dec_b + 7) // 8)
    cond = jnp.zeros((cond_rows, d), F32).at[0].set(c_ctx).at[1:1 + dec_b].set(c)
    mod = _ada(cond, w_ada, b_ada)

    y_prompt = x_prompt
    pb, pt = x_prompt.shape[:2]
    zero_state = jnp.zeros((pb,) + state_hgrn.shape[2:], F32)
    new = tuple(jnp.zeros((pb, depth, pt, width), F32)
                for width in (BRANCH_W, BRANCH_W, WIN_KV_HEADS * LANES, WIN_KV_HEADS * LANES))
    new += (jnp.zeros((pb, depth) + state_hgrn.shape[2:], F32),)
    for l in range(depth):
        y_prompt, new = _layer(y_prompt, mod[l, 0:1], l, w, None, None, zero_state, new)

    ropes = _rope_tables(dec_t, DIFF_QK_DIM) + _rope_tables(dec_t, LANES)
    y_sample = x_sample
    for l in range(depth):
        caches = (cache_diff_k[:, l], cache_diff_v[:, l], cache_win_k[:, l], cache_win_v[:, l])
        y_sample, _ = _layer(y_sample, mod[l, 1:1 + dec_b], l, w, ropes, caches, state_hgrn[:, l], None)

    return (y_prompt, y_sample,
            new[0].reshape(pb, depth, pt, N_HEADS, 2, DIFF_QK_DIM),
            new[1].reshape(pb, depth, pt, N_HEADS, LANES),
            new[2].reshape(pb, depth, pt, WIN_KV_HEADS, LANES),
            new[3].reshape(pb, depth, pt, WIN_KV_HEADS, LANES),
            new[4])
```

```python
import functools
import math

import numpy as np
import jax
import jax.numpy as jnp
from jax import lax
from jax.experimental import pallas as pl
from jax.experimental.pallas import tpu as pltpu

F32 = jnp.float32
BF16 = jnp.bfloat16

NORM_EPS = 1e-6
F_FLOOR = 1e-30
ROPE_BASE = 10000.0
GRID_W = 64
WINDOW = 128

LANES = 128
N_HEADS = 8
WIN_KV_HEADS = 4
WIN_GROUP = 2
DIFF_QK_DIM = 64
BRANCH_W = N_HEADS * LANES
HG_CHUNK = 64
HG_LEVELS = 6
HG_MXU_LEVELS = 3
DIFF_KV_CHUNK = 512
WIN_BLOCKS_PER_STEP = 16
DIFF_Q_TILE = 2048
LOG2E = math.log2(math.e)

OFF_DQ, OFF_DK, OFF_DV, OFF_DG = 0, 1024, 2048, 3072
OFF_HQ, OFF_HFF, OFF_HFB, OFF_HI, OFF_HG = 4096, 5120, 6144, 7168, 8192
OFF_WQ, OFF_WK, OFF_WV, OFF_WG = 9216, 10240, 10752, 11264
OFF_MG = 12288

VMEM_LIMIT = 48 * 1024 * 1024
VMEM_LIMIT_WIDE = 58 * 1024 * 1024

NT_DIMS = (((1,), (1,)), ((), ()))
TN_DIMS = (((0,), (0,)), ((), ()))


def _params(*sem, vmem=VMEM_LIMIT):
    return pltpu.CompilerParams(dimension_semantics=sem, vmem_limit_bytes=vmem)


def _tile(n, preferred):
    t = min(n, preferred)
    while n % t:
        t //= 2
    return t


def _dot(a, b):
    return jnp.dot(a, b, preferred_element_type=F32)


def _dot_nt(a, b):
    return lax.dot_general(a, b, NT_DIMS, preferred_element_type=F32)


def _sigmoid(x):
    return 0.5 * jnp.tanh(0.5 * x) + 0.5


def _silu(x):
    return x * _sigmoid(x)


def _split_bf16(x, terms):
    out = []
    for _ in range(terms - 1):
        hi = x.astype(BF16)
        out.append(hi)
        x = x - hi.astype(F32)
    out.append(x.astype(BF16))
    return out


def _ada_kernel(c_ref, w_ref, b_ref, o_ref):
    a = _silu(c_ref[...])
    a_hi, a_lo = _split_bf16(a, 2)
    w_hi, w_lo = _split_bf16(w_ref[...], 2)
    acc = _dot(a_hi, w_hi) + _dot(a_lo, w_hi) + _dot(a_hi, w_lo)
    o_ref[...] = acc + b_ref[...]


def _ada(cond, w_ada, b_ada):
    depth, d, n = w_ada.shape
    tn = _tile(n, 512)
    rows = cond.shape[0]
    return pl.pallas_call(
        _ada_kernel,
        grid=(depth, n // tn),
        in_specs=[
            pl.BlockSpec((rows, d), lambda l, j: (0, 0)),
            pl.BlockSpec((None, d, tn), lambda l, j: (l, 0, j)),
            pl.BlockSpec((None, 1, tn), lambda l, j: (l, 0, j)),
        ],
        out_specs=pl.BlockSpec((None, rows, tn), lambda l, j: (l, 0, j)),
        out_shape=jax.ShapeDtypeStruct((depth, rows, n), F32),
        compiler_params=_params("parallel", "parallel"),
        name="ada_mod",
    )(cond, w_ada, b_ada.reshape(depth, 1, n))


def _in_proj_kernel(x_ref, shift_ref, scale_ref, gain_ref, w_ref, o_ref, h_ref):
    @pl.when(pl.program_id(1) == 0)
    def _():
        x = x_ref[...]
        ms = jnp.mean(x * x, axis=-1, keepdims=True)
        y = x * lax.rsqrt(ms + NORM_EPS) * gain_ref[...]
        h_ref[...] = (y * (1.0 + scale_ref[...]) + shift_ref[...]).astype(BF16)

    o_ref[...] = _dot(h_ref[...], w_ref[...])


def _in_proj(x2d, shift, scale, gain, w_in, layer, rows_per_cond):
    m, d = x2d.shape
    n = w_in.shape[-1]
    tm = _tile(rows_per_cond, 1024)
    tn = _tile(n, 1024)
    return pl.pallas_call(
        _in_proj_kernel,
        grid=(m // tm, n // tn),
        in_specs=[
            pl.BlockSpec((tm, d), lambda i, j: (i, 0)),
            pl.BlockSpec((None, 1, d), lambda i, j: (i * tm // rows_per_cond, 0, 0)),
            pl.BlockSpec((None, 1, d), lambda i, j: (i * tm // rows_per_cond, 0, 0)),
            pl.BlockSpec((1, d), lambda i, j: (0, 0)),
            pl.BlockSpec((None, d, tn), lambda i, j: (layer, 0, j)),
        ],
        out_specs=pl.BlockSpec((tm, tn), lambda i, j: (i, j)),
        out_shape=jax.ShapeDtypeStruct((m, n), F32),
        scratch_shapes=[pltpu.VMEM((tm, d), BF16)],
        compiler_params=_params("parallel", "arbitrary"),
        name="in_proj",
    )(x2d, shift, scale, gain, w_in)


def _head_norm(x, gain, group):
    x2 = x * x
    if group == LANES:
        ms = jnp.sum(x2, axis=-1, keepdims=True) * (1.0 / LANES)
    else:
        lo = lax.broadcasted_iota(jnp.int32, x.shape, 1) < group
        s_lo = jnp.sum(jnp.where(lo, x2, 0.0), axis=-1, keepdims=True)
        s_hi = jnp.sum(jnp.where(lo, 0.0, x2), axis=-1, keepdims=True)
        ms = jnp.where(lo, s_lo, s_hi) * (1.0 / group)
    return x * lax.rsqrt(ms + NORM_EPS) * gain


PREP_W = 2 * LANES


def _group_mean_matrix(group):
    i = np.arange(PREP_W)
    return ((i[:, None] // group) == (i[None, :] // group)).astype(np.float32) / group


def _rotate_half_matrix(d):
    quarter = d // 4
    i = np.arange(PREP_W)
    first = (i % (2 * quarter)) < quarter
    mat = np.zeros((PREP_W, PREP_W), np.float32)
    mat[i[first] + quarter, i[first]] = -1.0
    mat[i[~first] - quarter, i[~first]] = 1.0
    return mat


def _prep_kernel(*refs, rope, keep_f32):
    it = iter(refs)
    dq_ref, dk_ref, dv_ref, wq_ref, wk_ref, wv_ref = (next(it) for _ in range(6))
    gdq_ref, gdk_ref, gwq_ref, gwk_ref, md_ref, mw_ref = (next(it) for _ in range(6))
    if rope:
        rd_ref, rw_ref, cd_ref, sd_ref, cw_ref, sw_ref = (next(it) for _ in range(6))
    if keep_f32:
        for _ in range(4):
            next(it)
    odq, odk, odv, owq, owk, owv = (next(it) for _ in range(6))
    if keep_f32:
        odk32, odv32, owk32, owv32 = (next(it) for _ in range(4))

    def pair(src, j, gain_ref, mean_ref, rot_ref, cos_ref, sin_ref):
        x = src[:, j * PREP_W:(j + 1) * PREP_W]
        hi, lo = _split_bf16(x * x, 2)
        ms = _dot(hi, mean_ref[...]) + _dot(lo, mean_ref[...])
        y = x * lax.rsqrt(ms + NORM_EPS) * gain_ref[...]
        if rope:
            y = y * cos_ref[...] + _dot(y.astype(BF16), rot_ref[...]) * sin_ref[...]
        return y

    dtabs = (rd_ref, cd_ref, sd_ref) if rope else (None, None, None)
    wtabs = (rw_ref, cw_ref, sw_ref) if rope else (None, None, None)
    for j in range(N_HEADS // 2):
        sl = slice(j * PREP_W, (j + 1) * PREP_W)
        q = pair(dq_ref, j, gdq_ref, md_ref, *dtabs)
        odq[:, sl] = (q * (DIFF_QK_DIM ** -0.5 * LOG2E)).astype(BF16)
        k = pair(dk_ref, j, gdk_ref, md_ref, *dtabs)
        odk[:, sl] = k.astype(BF16)
        if keep_f32:
            odk32[:, sl] = k
        q = pair(wq_ref, j, gwq_ref, mw_ref, *wtabs)
        owq[:, sl] = (q * (LANES ** -0.5 * LOG2E)).astype(BF16)
    for j in range(WIN_KV_HEADS // 2):
        sl = slice(j * PREP_W, (j + 1) * PREP_W)
        k = pair(wk_ref, j, gwk_ref, mw_ref, *wtabs)
        owk[:, sl] = k.astype(BF16)
        if keep_f32:
            owk32[:, sl] = k
    for j in range(N_HEADS):
        odv[j] = dv_ref[:, j * LANES:(j + 1) * LANES].T.astype(BF16)
    for j in range(WIN_KV_HEADS):
        for u in range(owv.shape[1]):
            owv[j, u] = wv_ref[u * WINDOW:(u + 1) * WINDOW, j * LANES:(j + 1) * LANES].T.astype(BF16)
    if keep_f32:
        odv32[...] = dv_ref[...]
        owv32[...] = wv_ref[...]


def _prep(proj, gains, rope_tabs, seq_len, caches=None, layer=0):
    m = proj.shape[0]
    keep_f32 = caches is not None
    tr = _tile(seq_len, DIFF_KV_CHUNK)
    nt = seq_len // tr
    w8, w4 = BRANCH_W, WIN_KV_HEADS * LANES
    rope = rope_tabs is not None

    def col(width, off):
        return pl.BlockSpec((tr, width), lambda i: (i, off // width))

    small = pl.BlockSpec((1, PREP_W), lambda i: (0, 0))
    square = pl.BlockSpec((PREP_W, PREP_W), lambda i: (0, 0))
    in_specs = [col(w8, OFF_DQ), col(w8, OFF_DK), col(w8, OFF_DV), col(w8, OFF_WQ),
                col(w4, OFF_WK), col(w4, OFF_WV), small, small, small, small, square, square]
    args = [proj] * 6 + list(gains) + [jnp.asarray(_group_mean_matrix(DIFF_QK_DIM), BF16),
                                        jnp.asarray(_group_mean_matrix(LANES), BF16)]
    if rope:
        tab = pl.BlockSpec((tr, PREP_W), lambda i: (i % nt, 0))
        in_specs += [square, square] + [tab] * 4
        args += [jnp.asarray(_rotate_half_matrix(DIFF_QK_DIM), BF16),
                 jnp.asarray(_rotate_half_matrix(LANES), BF16)] + list(rope_tabs)
    out8 = pl.BlockSpec((tr, w8), lambda i: (i, 0))
    out4 = pl.BlockSpec((tr, w4), lambda i: (i, 0))
    out_vt = pl.BlockSpec((None, N_HEADS, None, LANES, tr), lambda i: (i // nt, 0, i % nt, 0, 0))
    kb = tr // WINDOW
    out_wvt = pl.BlockSpec((None, WIN_KV_HEADS, kb, LANES, WINDOW), lambda i: (i // nt, 0, i % nt, 0, 0))
    out_specs = [out8, out8, out_vt, out8, out4, out_wvt]
    out_shape = [jax.ShapeDtypeStruct((m, w8), BF16), jax.ShapeDtypeStruct((m, w8), BF16),
                 jax.ShapeDtypeStruct((m // seq_len, N_HEADS, nt, LANES, tr), BF16),
                 jax.ShapeDtypeStruct((m, w8), BF16), jax.ShapeDtypeStruct((m, w4), BF16),
                 jax.ShapeDtypeStruct((m // seq_len, WIN_KV_HEADS, seq_len // WINDOW, LANES, WINDOW), BF16)]
    aliases = {}
    if keep_f32:
        for j, cache in enumerate(caches):
            aliases[len(args)] = len(out_shape)
            in_specs.append(pl.BlockSpec(memory_space=pl.ANY))
            args.append(cache)
            out_specs.append(pl.BlockSpec((None, None, tr, cache.shape[-1]),
                                          lambda i: (i // nt, layer, i % nt, 0)))
            out_shape.append(jax.ShapeDtypeStruct(cache.shape, F32))
    return pl.pallas_call(
        functools.partial(_prep_kernel, rope=rope, keep_f32=keep_f32),
        grid=(m // tr,),
        in_specs=in_specs,
        out_specs=out_specs,
        out_shape=out_shape,
        input_output_aliases=aliases,
        compiler_params=_params("parallel"),
        name="qk_prep",
    )(*args)


def _diff_attn_kernel(*refs, lam_init, n_chunks, ck, has_ctx):
    it = iter(refs)
    lam_ref, subln_ref, q_ref, k_ref, vt_ref = (next(it) for _ in range(5))
    if has_ctx:
        kc_ref, vct_ref = next(it), next(it)
    g_ref, o_ref, acc_ref = next(it), next(it), next(it)

    tq = q_ref.shape[0]
    lv = lam_ref[...]
    lam = (jnp.exp(jnp.sum(lv[0:1] * lv[1:2], axis=-1, keepdims=True))
           - jnp.exp(jnp.sum(lv[2:3] * lv[3:4], axis=-1, keepdims=True)) + lam_init)
    first = lax.broadcasted_iota(jnp.int32, (tq, LANES), 1) < DIFF_QK_DIM

    for hh in range(vt_ref.shape[0]):
        sl = slice(hh * LANES, (hh + 1) * LANES)
        q = q_ref[:, sl]
        zero = jnp.zeros_like(q)
        q2 = jnp.concatenate([jnp.where(first, q, zero), jnp.where(first, zero, q)], axis=0)

        def step(kb, vtb, carry, is_first, hh=hh, q2=q2):
            m, l = carry
            s = _dot_nt(kb, q2)
            m_new = jnp.maximum(m, jnp.max(s, axis=0, keepdims=True))
            p = jnp.exp2(s - m_new)
            pv = _dot(vtb, p.astype(BF16))
            if is_first:
                l = jnp.sum(p, axis=0, keepdims=True)
                acc_ref[hh] = pv
            else:
                alpha = jnp.exp2(m - m_new)
                l = alpha * l + jnp.sum(p, axis=0, keepdims=True)
                acc_ref[hh] = alpha * acc_ref[hh] + pv
            return m_new, l

        none = jnp.zeros((1, 2 * tq), F32)
        carry = step(k_ref[pl.ds(0, ck), sl], vt_ref[hh, 0], (none, none), True)

        def body(c, carry, hh=hh, sl=sl, step=step):
            r = pl.multiple_of(c * ck, ck)
            return step(k_ref[pl.ds(r, ck), sl], vt_ref[hh, c], carry, False)

        carry = lax.fori_loop(1, n_chunks, body, carry)
        if has_ctx:
            carry = step(kc_ref[:, sl], vct_ref[hh], carry, False)

        r = acc_ref[hh] / carry[1]
        o = (r[:, :tq] - lam * r[:, tq:]).T
        ms = jnp.mean(o * o, axis=-1, keepdims=True)
        y = o * lax.rsqrt(ms + NORM_EPS) * subln_ref[...]
        y = y * (1.0 - lam_init)
        o_ref[:, sl] = (y * _silu(g_ref[:, sl])).astype(BF16)


def _diff_attn(q, k, vt, ctx, proj3, lam_vecs, subln, lam_init):
    b, t, _ = q.shape
    n_chunks, ck = vt.shape[2], vt.shape[4]
    tq = _tile(t, DIFF_Q_TILE)
    hps = max(1, min(N_HEADS, DIFF_Q_TILE // t))
    hw = hps * LANES
    has_ctx = ctx is not None
    qspec = pl.BlockSpec((None, tq, hw), lambda bi, h, i: (bi, i, h))
    in_specs = [pl.BlockSpec((4, DIFF_QK_DIM), lambda bi, h, i: (0, 0)),
                pl.BlockSpec((1, LANES), lambda bi, h, i: (0, 0)),
                qspec,
                pl.BlockSpec((None, t, hw), lambda bi, h, i: (bi, 0, h)),
                pl.BlockSpec((None, hps, n_chunks, LANES, ck), lambda bi, h, i: (bi, h, 0, 0, 0))]
    args = [lam_vecs, subln, q, k, vt]
    if has_ctx:
        p = ctx[0].shape[1]
        in_specs += [pl.BlockSpec((None, p, hw), lambda bi, h, i: (bi, 0, h)),
                     pl.BlockSpec((None, hps, LANES, p), lambda bi, h, i: (bi, h, 0, 0))]
        args += list(ctx)
    in_specs.append(pl.BlockSpec((None, tq, hw), lambda bi, h, i: (bi, i, OFF_DG // hw + h)))
    args.append(proj3)
    return pl.pallas_call(
        functools.partial(_diff_attn_kernel, lam_init=lam_init, n_chunks=n_chunks, ck=ck, has_ctx=has_ctx),
        grid=(b, N_HEADS // hps, t // tq),
        in_specs=in_specs,
        out_specs=qspec,
        out_shape=jax.ShapeDtypeStruct((b, t, BRANCH_W), BF16),
        scratch_shapes=[pltpu.VMEM((hps, LANES, 2 * tq), F32)],
        compiler_params=_params("parallel", "parallel", "parallel"),
        name="diff_attn",
    )(*args)


def _win_attn_kernel(*refs, banded, n_blocks, blocks_per_step, has_ctx):
    it = iter(refs)
    sink_ref, q_ref, k_ref, vt_ref = (next(it) for _ in range(4))
    if has_ctx:
        kc_ref, vct_ref = next(it), next(it)
    g_ref, o_ref = next(it), next(it)

    kvh = pl.program_id(1)
    w = WINDOW
    gw = WIN_GROUP * w
    lane = lax.broadcasted_iota(jnp.int32, (1, gw), 1)
    sink = jnp.where(lane < w, sink_ref[kvh * WIN_GROUP], sink_ref[kvh * WIN_GROUP + 1]) * LOG2E
    if banded:
        key = lax.broadcasted_iota(jnp.int32, (w, gw), 0)
        qry = lax.broadcasted_iota(jnp.int32, (w, gw), 1) & (w - 1)
    shared = []
    if not banded:
        shared += [(k_ref[j * w:(j + 1) * w, :], vt_ref[j], None) for j in range(n_blocks)]

    q2s = [jnp.concatenate([q_ref[u * w:(u + 1) * w, j * LANES:(j + 1) * LANES] for j in range(WIN_GROUP)],
                           axis=0) for u in range(blocks_per_step)]
    if has_ctx:
        s_ctx = _dot_nt(kc_ref[...], jnp.concatenate(q2s, axis=0))
    stats, p_ctx = [], []

    for u in range(blocks_per_step):
        cols = slice(u * gw, (u + 1) * gw)
        q2 = q2s[u]
        segs = []
        if banded:
            n = pl.program_id(2) * blocks_per_step + u
            b_prev = jnp.maximum(n - 1, 0)
            b_next = jnp.minimum(n + 1, n_blocks - 1)
            keep_prev = (key >= qry) & (n > 0)
            keep_next = (key <= qry) & (n < n_blocks - 1)
            segs = [(k_ref[pl.ds(pl.multiple_of(b_prev * w, w), w), :], vt_ref[b_prev], keep_prev),
                    (k_ref[pl.ds(pl.multiple_of(n * w, w), w), :], vt_ref[n], None),
                    (k_ref[pl.ds(pl.multiple_of(b_next * w, w), w), :], vt_ref[b_next], keep_next)]
        segs += shared
        scores = []
        for kb, _, keep in segs:
            s = _dot_nt(kb, q2)
            scores.append(s if keep is None else jnp.where(keep, s, -jnp.inf))
        m = sink
        for s in scores:
            m = jnp.maximum(m, jnp.max(s, axis=0, keepdims=True))
        if has_ctx:
            m = jnp.maximum(m, jnp.max(s_ctx[:, cols], axis=0, keepdims=True))
        l = jnp.exp2(sink - m)
        acc = None
        for s, (_, vtb, _) in zip(scores, segs):
            p = jnp.exp2(s - m)
            l = l + jnp.sum(p, axis=0, keepdims=True)
            pv = _dot(vtb, p.astype(BF16))
            acc = pv if acc is None else acc + pv
        if has_ctx:
            p = jnp.exp2(s_ctx[:, cols] - m)
            l = l + jnp.sum(p, axis=0, keepdims=True)
            p_ctx.append(p.astype(BF16))
        stats.append((acc, l))

    if has_ctx:
        pv_ctx = _dot(vct_ref[...], jnp.concatenate(p_ctx, axis=1))
    for u in range(blocks_per_step):
        rows = slice(u * w, (u + 1) * w)
        acc, l = stats[u]
        if has_ctx:
            acc = acc + pv_ctx[:, u * gw:(u + 1) * gw]
        o = (acc / l).T
        for j in range(WIN_GROUP):
            sl = slice(j * LANES, (j + 1) * LANES)
            o_ref[rows, sl] = (o[j * w:(j + 1) * w] * _silu(g_ref[rows, sl])).astype(BF16)


def _win_attn(q, k, vt, ctx, proj3, sink, banded):
    b, t, _ = q.shape
    nb = t // WINDOW
    gw = WIN_GROUP * LANES
    has_ctx = ctx is not None
    per_step = _tile(nb, WIN_BLOCKS_PER_STEP)
    tq = per_step * WINDOW
    qspec = pl.BlockSpec((None, tq, gw), lambda bi, h, i: (bi, i, h))
    in_specs = [pl.BlockSpec(memory_space=pltpu.SMEM), qspec,
                pl.BlockSpec((None, t, LANES), lambda bi, h, i: (bi, 0, h)),
                pl.BlockSpec((None, None, nb, LANES, WINDOW), lambda bi, h, i: (bi, h, 0, 0, 0))]
    args = [sink, q, k, vt]
    if has_ctx:
        p = ctx[0].shape[1]
        in_specs += [pl.BlockSpec((None, p, LANES), lambda bi, h, i: (bi, 0, h)),
                     pl.BlockSpec((None, None, LANES, p), lambda bi, h, i: (bi, h, 0, 0))]
        args += list(ctx)
    in_specs.append(pl.BlockSpec((None, tq, gw), lambda bi, h, i: (bi, i, OFF_WG // gw + h)))
    args.append(proj3)
    return pl.pallas_call(
        functools.partial(_win_attn_kernel, banded=banded, n_blocks=nb, blocks_per_step=per_step,
                          has_ctx=has_ctx),
        grid=(b, WIN_KV_HEADS, nb // per_step),
        in_specs=in_specs,
        out_specs=qspec,
        out_shape=jax.ShapeDtypeStruct((b, t, BRANCH_W), BF16),
        compiler_params=_params("parallel", "parallel", "parallel"),
        name="win_attn",
    )(*args)


def _hgrn_tables(reverse):
    c = HG_CHUNK
    t = np.arange(c)[:, None]
    u = np.arange(c)[None, :]
    mats, masks = [], []
    for m in range(HG_LEVELS):
        half, blk = 1 << m, 2 << m
        r = t - t % blk + half - 1
        right = (t % blk) >= half
        if not reverse:
            e_q = (u >= r + 1) & (u <= t)
            e_k = (u >= t + 1) & (u <= r)
            mats.append(np.where(right, e_q, e_k))
            masks.append(right & ~right.T & (t // blk == u // blk))
        else:
            e_q = (u >= t) & (u <= r)
            e_k = (u >= r + 1) & (u <= t - 1)
            mats.append(np.where(right, e_k, e_q))
            masks.append(~right & right.T & (t // blk == u // blk))
    mats = mats[:HG_MXU_LEVELS] + [(u <= t) if not reverse else (u >= t)]
    masks.append(t == u)
    return (np.concatenate(mats, 0).astype(np.float32), np.stack(masks).astype(np.float32))


def _hgrn_constants():
    c = HG_CHUNK
    tabs = [_hgrn_tables(False), _hgrn_tables(True)]
    mats = np.stack([np.tile(tb[0], (1, 3)) for tb in tabs])
    masks = np.zeros((HG_LEVELS + 1, 4, c, LANES), np.float32)
    for a in range(4):
        lo = (a % 2) * c
        masks[:, a, :, lo:lo + c] = tabs[a // 2][1]
    return mats, masks


def _hgrn_kernel(lb_ref, s0_ref, mat_ref, mask_ref,
                 qf_ref, vf_ref, zf_ref, qb_ref, vb_ref, zb_ref, *rest, layer, n_chunks):
    of_ref, ob_ref, sout_ref, st_ref = rest[-4:]
    i = pl.program_id(2)
    c = HG_CHUNK
    pair = 2
    pw = pair * LANES

    @pl.when(i == 0)
    def _():
        for d in range(2):
            for hh in range(pair):
                st_ref[2 * d + hh] = s0_ref[d, hh]

    def lower_bound(d):
        rows = [lb_ref[d, r:r + 1, :] for r in range(lb_ref.shape[1])]
        mx = functools.reduce(jnp.maximum, rows)
        es = [jnp.exp(r - mx) for r in rows]
        den = functools.reduce(lambda a, b_: a + b_, es)
        ps = [e / den for e in es]
        return functools.reduce(lambda a, b_: a + b_, ps[:layer + 1]) - ps[0]

    bit_row = lax.broadcasted_iota(jnp.int32, (c, pw), 0)

    def side_rows(m, d, on_query_rows, on_key_rows):
        half = 1 << m
        if half % 8:
            return jnp.where(((bit_row >> m) & 1) == (1 - d), on_query_rows, on_key_rows)
        slabs = [(on_query_rows if (r0 // half) % 2 == 1 - d else on_key_rows)[r0:r0 + half]
                 for r0 in range(0, c, half)]
        return jnp.concatenate(slabs, axis=0)

    def decays(d, z, lb):
        e = jnp.exp(-jnp.abs(z))
        r = 1.0 / (1.0 + e)
        er = e * r
        pos = z >= 0.0
        f = lb + (1.0 - lb) * jnp.where(pos, r, er)
        kk = (1.0 - lb) * jnp.where(pos, er, r)
        logf = jnp.log(jnp.maximum(f, F_FLOOR)) * LOG2E
        parts = jnp.concatenate(_split_bf16(logf, 3), axis=0)
        e_small = _dot(mat_ref[d], parts)
        run = e_small[HG_MXU_LEVELS * c:]
        levels = [jnp.exp2(e_small[m * c:(m + 1) * c]) for m in range(HG_MXU_LEVELS)]
        for m in range(HG_MXU_LEVELS, HG_LEVELS):
            half, blk = 1 << m, 2 << m
            edge = half - 1 + d
            rho = jnp.concatenate(
                [jnp.broadcast_to(run[b0 + edge:b0 + edge + 1], (blk, pw)) for b0 in range(0, c, blk)], axis=0)
            diff = run - rho
            levels.append(jnp.exp2(side_rows(m, d, diff, -diff)))
        last = (c - 1) if d == 0 else 0
        total = run[last:last + 1]
        return kk, levels, jnp.exp2(run), jnp.exp2(total - run), jnp.exp2(total)

    lbs = (lower_bound(0), lower_bound(1))

    def body(ci, _):
        rows = (pl.ds(pl.multiple_of(ci * c, c), c), pl.ds(pl.multiple_of((n_chunks - 1 - ci) * c, c), c))
        qs = (qf_ref[rows[0], :], qb_ref[rows[1], :])
        vs = (vf_ref[rows[0], :].astype(BF16), vb_ref[rows[1], :].astype(BF16))
        zs = (zf_ref[rows[0], :], zb_ref[rows[1], :])
        dec = [decays(d, zs[d], lbs[d]) for d in range(2)]

        def stacked(fn):
            wide = [fn(d).astype(BF16) for d in range(2)]
            return jnp.concatenate([wide[a // 2][:, (a % 2) * LANES:(a % 2 + 1) * LANES] for a in range(4)],
                                   axis=0)

        def pair_weights(g, m):
            return [mask_ref[m, a] * g[a * c:(a + 1) * c, (a // 2) * LANES:(a // 2 + 1) * LANES]
                    for a in range(4)]

        acc = pair_weights(_dot_nt(stacked(lambda d: qs[d]), stacked(lambda d: dec[d][0])), HG_LEVELS)
        for m in range(HG_LEVELS):
            def level_rows(d, m=m):
                return side_rows(m, d, qs[d], dec[d][0]) * dec[d][1][m]
            lm = stacked(level_rows)
            acc = [x + y for x, y in zip(acc, pair_weights(_dot_nt(lm, lm), m))]

        outs = (of_ref, ob_ref)
        for a in range(4):
            d, hh = a // 2, a % 2
            sl = slice(hh * LANES, (hh + 1) * LANES)
            kk, _, from_start, to_end, whole = dec[d]
            v = vs[d][:, sl]
            qe = (qs[d][:, sl] * from_start[:, sl]).astype(BF16)
            ke = (kk[:, sl] * to_end[:, sl]).astype(BF16)
            st = st_ref[a]
            lhs = jnp.concatenate([acc[a].astype(BF16), qe], axis=1)
            rhs = jnp.concatenate([v, v, st.astype(BF16)], axis=0)
            outs[d][rows[d], sl] = _dot(lhs, rhs)
            decay_col = jnp.broadcast_to(whole[:, sl], (LANES, LANES)).T
            st_ref[a] = st * decay_col + lax.dot_general(ke, v, TN_DIMS, preferred_element_type=F32)
        return 0

    lax.fori_loop(0, n_chunks, body, 0, unroll=8)

    @pl.when(i == pl.num_programs(2) - 1)
    def _():
        for d in range(2):
            for hh in range(pair):
                sout_ref[d, hh] = st_ref[2 * d + hh]


def _hgrn(proj3, hg_lb, state0, layer, states_out=None):
    b, t, _ = proj3.shape
    tb = _tile(t, 1024)
    nt = t // tb
    pw = 2 * LANES
    mats, masks = _hgrn_constants()
    mats = jnp.asarray(mats, BF16)
    masks = jnp.asarray(masks, F32)

    def fwd(off):
        return pl.BlockSpec((None, tb, pw), lambda bi, hp, i: (bi, i, off // pw + hp))

    def bwd(off):
        return pl.BlockSpec((None, tb, pw), lambda bi, hp, i: (bi, nt - 1 - i, off // pw + hp))

    st_spec = pl.BlockSpec((None, 2, 2, LANES, LANES), lambda bi, hp, i: (bi, 0, hp, 0, 0))
    in_specs = [
        pl.BlockSpec((2, hg_lb.shape[1], pw), lambda bi, hp, i: (0, 0, hp)),
        st_spec,
        pl.BlockSpec(mats.shape, lambda bi, hp, i: (0, 0, 0)),
        pl.BlockSpec(masks.shape, lambda bi, hp, i: (0, 0, 0, 0)),
        fwd(OFF_HQ), fwd(OFF_HI), fwd(OFF_HFF),
        bwd(OFF_HQ), bwd(OFF_HI), bwd(OFF_HFB),
    ]
    args = [hg_lb, state0, mats, masks, proj3, proj3, proj3, proj3, proj3, proj3]
    aliases = {}
    if states_out is None:
        st_out_spec, st_out_shape = st_spec, jax.ShapeDtypeStruct(state0.shape, F32)
    else:
        aliases[len(args)] = 2
        in_specs.append(pl.BlockSpec(memory_space=pl.ANY))
        args.append(states_out)
        st_out_spec = pl.BlockSpec((None, None, 2, 2, LANES, LANES), lambda bi, hp, i: (bi, layer, 0, hp, 0, 0))
        st_out_shape = jax.ShapeDtypeStruct(states_out.shape, F32)
    return pl.pallas_call(
        functools.partial(_hgrn_kernel, layer=layer, n_chunks=tb // HG_CHUNK),
        grid=(b, N_HEADS // 2, nt),
        in_specs=in_specs,
        out_specs=[
            pl.BlockSpec((None, tb, pw), lambda bi, hp, i: (bi, i, hp)),
            pl.BlockSpec((None, tb, pw), lambda bi, hp, i: (bi, nt - 1 - i, hp)),
            st_out_spec,
        ],
        out_shape=[
            jax.ShapeDtypeStruct((b, t, BRANCH_W), F32),
            jax.ShapeDtypeStruct((b, t, BRANCH_W), F32),
            st_out_shape,
        ],
        scratch_shapes=[pltpu.VMEM((4, LANES, LANES), F32)],
        input_output_aliases=aliases,
        compiler_params=_params("parallel", "parallel", "arbitrary"),
        name="hgrn_scan",
    )(*args)


def _merge_kernel(yd_ref, of_ref, ob_ref, yw_ref, hg_ref, gain_ref, mg0_ref, mg1_ref, mg2_ref, w_ref,
                  o_ref, yh_ref):
    @pl.when(pl.program_id(1) == 0)
    def _():
        for j in range(N_HEADS):
            sl = slice(j * LANES, (j + 1) * LANES)
            y = _head_norm(of_ref[:, sl] + ob_ref[:, sl], gain_ref[...], LANES)
            yh_ref[:, sl] = (y * _silu(hg_ref[:, sl])).astype(BF16)

    acc = _sigmoid(mg0_ref[...]) * _dot(yd_ref[...], w_ref[0])
    acc = acc + _sigmoid(mg1_ref[...]) * _dot(yh_ref[...], w_ref[1])
    acc = acc + _sigmoid(mg2_ref[...]) * _dot(yw_ref[...], w_ref[2])
    o_ref[...] = acc.astype(BF16)


def _merge(yd, o_f, o_b, yw, proj, hg_gain, w_branch, layer):
    m = yd.shape[0]
    d = w_branch.shape[-1]
    tm = _tile(m, 1024)
    tn = _tile(d, 512)
    row = lambda width: pl.BlockSpec((tm, width), lambda i, j: (i, 0))

    def mg(nb):
        return pl.BlockSpec((tm, tn), lambda i, j: (i, (OFF_MG + nb * d) // tn + j))

    return pl.pallas_call(
        _merge_kernel,
        grid=(m // tm, d // tn),
        in_specs=[row(BRANCH_W), row(BRANCH_W), row(BRANCH_W), row(BRANCH_W),
                  pl.BlockSpec((tm, BRANCH_W), lambda i, j: (i, OFF_HG // BRANCH_W)),
                  pl.BlockSpec((1, LANES), lambda i, j: (0, 0)),
                  mg(0), mg(1), mg(2),
                  pl.BlockSpec((None, 3, BRANCH_W, tn), lambda i, j: (layer, 0, 0, j))],
        out_specs=pl.BlockSpec((tm, tn), lambda i, j: (i, j)),
        out_shape=jax.ShapeDtypeStruct((m, d), BF16),
        scratch_shapes=[pltpu.VMEM((tm, BRANCH_W), BF16)],
        compiler_params=_params("parallel", "arbitrary", vmem=VMEM_LIMIT_WIDE),
        name="branch_merge",
    )(yd, o_f, o_b, yw, proj, hg_gain, proj, proj, proj, w_branch)


def _out_proj_kernel(x_ref, gate_ref, m_ref, w_ref, o_ref):
    o_ref[...] = x_ref[...] + gate_ref[...] * _dot(m_ref[...], w_ref[...])


def _out_proj(x2d, gate, merged, w_out, layer, rows_per_cond):
    m, d = x2d.shape
    tm = _tile(rows_per_cond, 2048)
    tn = _tile(d, 512)
    return pl.pallas_call(
        _out_proj_kernel,
        grid=(m // tm, d // tn),
        in_specs=[
            pl.BlockSpec((tm, tn), lambda i, j: (i, j)),
            pl.BlockSpec((None, 1, tn), lambda i, j: (i * tm // rows_per_cond, 0, j)),
            pl.BlockSpec((tm, d), lambda i, j: (i, 0)),
            pl.BlockSpec((None, d, tn), lambda i, j: (layer, 0, j)),
        ],
        out_specs=pl.BlockSpec((tm, tn), lambda i, j: (i, j)),
        out_shape=jax.ShapeDtypeStruct((m, d), F32),
        compiler_params=_params("parallel", "parallel"),
        name="out_proj",
    )(x2d, gate, merged, w_out)


def _rope_tables(t, d):
    rows = t // GRID_W
    row = jnp.repeat(jnp.arange(rows, dtype=F32), GRID_W)
    col = jnp.tile(jnp.arange(GRID_W, dtype=F32), rows)
    quarter = d // 4
    inv = ROPE_BASE ** (-jnp.arange(quarter, dtype=F32) / quarter)
    ar = row[:, None] * inv
    ac = col[:, None] * inv
    ang = jnp.concatenate([ar, ar, ac, ac], axis=-1)
    return tuple(jnp.tile(a, (1, PREP_W // d)) for a in (jnp.cos(ang), jnp.sin(ang)))


def _tile_gain(g, width=LANES):
    return jnp.tile(g, width // g.shape[0]).reshape(1, width)


def _layer(x, mod, l, w, rope_tabs, caches, state0, new_caches):
    is_context = new_caches is not None
    b, t, d = x.shape
    m = b * t
    rows_per_cond = m // mod.shape[0]
    x2d = x.reshape(m, d)
    shift, scale, gate = (mod[:, None, j * d:(j + 1) * d] for j in range(3))
    proj = _in_proj(x2d, shift, scale, w["norm_gain"][l].reshape(1, d), w["w_in"], l, rows_per_cond)
    gains = tuple(_tile_gain(w[name][l], PREP_W)
                  for name in ("diff_q_norm", "diff_k_norm", "win_q_norm", "win_k_norm"))
    prepped = _prep(proj, gains, rope_tabs, t, new_caches[:4] if is_context else None, l)
    dq, dk, wq, wk = (prepped[j].reshape(b, t, -1) for j in (0, 1, 3, 4))
    dv, wv = prepped[2], prepped[5]
    proj3 = proj.reshape(b, t, -1)
    lam_init = 0.8 - 0.6 * math.exp(-0.3 * l)
    if is_context:
        dctx = wctx = None
    else:
        ck_d, cv_d, ck_w, cv_w = caches
        dctx = (ck_d.reshape(b, -1, BRANCH_W).astype(BF16),
                jnp.transpose(cv_d, (0, 2, 3, 1)).astype(BF16))
        wctx = (ck_w.reshape(b, -1, WIN_KV_HEADS * LANES).astype(BF16),
                jnp.transpose(cv_w, (0, 2, 3, 1)).astype(BF16))
    yd = _diff_attn(dq, dk, dv, dctx, proj3, w["diff_lambda"][l], _tile_gain(w["diff_subln"][l]), lam_init)
    o_f, o_b, s_out = _hgrn(proj3, w["hg_lb"], state0, l, new_caches[4] if is_context else None)
    yw = _win_attn(wq, wk, wv, wctx, proj3, w["win_sink"][l], banded=not is_context)
    merged = _merge(yd.reshape(m, -1), o_f.reshape(m, -1), o_b.reshape(m, -1), yw.reshape(m, -1), proj,
                    _tile_gain(w["hg_out_norm"][l]), w["w_branch"], l)
    y = _out_proj(x2d, gate, merged, w["w_out"], l, rows_per_cond).reshape(b, t, d)
    if not is_context:
        return y, None
    return y, tuple(prepped[6:10]) + (s_out,)


def kernel(x_prompt, x_sample, cache_diff_k, cache_diff_v, cache_win_k, cache_win_v, state_hgrn, c, c_ctx, norm_gain, w_ada, b_ada, w_in, diff_q_norm, diff_k_norm, diff_lambda, diff_subln, hg_lb, hg_out_norm, win_q_norm, win_k_norm, win_sink, w_branch, w_out):
    depth = w_in.shape[0]
    dec_b, dec_t, d = x_sample.shape
    w = {"norm_gain": norm_gain, "w_in": w_in.astype(BF16), "diff_q_norm": diff_q_norm,
         "diff_k_norm": diff_k_norm, "diff_lambda": diff_lambda, "diff_subln": diff_subln,
         "hg_lb": hg_lb, "hg_out_norm": hg_out_norm, "win_q_norm": win_q_norm, "win_k_norm": win_k_norm,
         "win_sink": win_sink, "w_branch": w_branch.astype(BF16), "w_out": w_out.astype(BF16)}

    cond_rows = 8 * ((1 + dec_b + 7) // 8)
    cond = jnp.zeros((cond_rows, d), F32).at[0].set(c_ctx).at[1:1 + dec_b].set(c)
    mod = _ada(cond, w_ada, b_ada)

    y_prompt = x_prompt
    pb, pt = x_prompt.shape[:2]
    zero_state = jnp.zeros((pb,) + state_hgrn.shape[2:], F32)
    new = tuple(jnp.zeros((pb, depth, pt, width), F32)
                for width in (BRANCH_W, BRANCH_W, WIN_KV_HEADS * LANES, WIN_KV_HEADS * LANES))
    new += (jnp.zeros((pb, depth) + state_hgrn.shape[2:], F32),)
    for l in range(depth):
        y_prompt, new = _layer(y_prompt, mod[l, 0:1], l, w, None, None, zero_state, new)

    ropes = _rope_tables(dec_t, DIFF_QK_DIM) + _rope_tables(dec_t, LANES)
    y_sample = x_sample
    for l in range(depth):
        caches = (cache_diff_k[:, l], cache_diff_v[:, l], cache_win_k[:, l], cache_win_v[:, l])
        y_sample, _ = _layer(y_sample, mod[l, 1:1 + dec_b], l, w, ropes, caches, state_hgrn[:, l], None)

    return (y_prompt, y_sample,
            new[0].reshape(pb, depth, pt, N_HEADS, 2, DIFF_QK_DIM),
            new[1].reshape(pb, depth, pt, N_HEADS, LANES),
            new[2].reshape(pb, depth, pt, WIN_KV_HEADS, LANES),
            new[3].reshape(pb, depth, pt, WIN_KV_HEADS, LANES),
            new[4])
```

```python
import functools
import math

import numpy as np
import jax
import jax.numpy as jnp
from jax import lax
from jax.experimental import pallas as pl
from jax.experimental.pallas import tpu as pltpu

F32 = jnp.float32
BF16 = jnp.bfloat16

NORM_EPS = 1e-6
F_FLOOR = 1e-30
ROPE_BASE = 10000.0
GRID_W = 64
WINDOW = 128

LANES = 128
N_HEADS = 8
WIN_KV_HEADS = 4
WIN_GROUP = 2
DIFF_QK_DIM = 64
BRANCH_W = N_HEADS * LANES
HG_CHUNK = 64
HG_LEVELS = 6
HG_MXU_LEVELS = 3
DIFF_KV_CHUNK = 512
WIN_BLOCKS_PER_STEP = 16
DIFF_Q_TILE = 2048
LOG2E = math.log2(math.e)

OFF_DQ, OFF_DK, OFF_DV, OFF_DG = 0, 1024, 2048, 3072
OFF_HQ, OFF_HFF, OFF_HFB, OFF_HI, OFF_HG = 4096, 5120, 6144, 7168, 8192
OFF_WQ, OFF_WK, OFF_WV, OFF_WG = 9216, 10240, 10752, 11264
OFF_MG = 12288

VMEM_LIMIT = 48 * 1024 * 1024
VMEM_LIMIT_WIDE = 58 * 1024 * 1024

NT_DIMS = (((1,), (1,)), ((), ()))
TN_DIMS = (((0,), (0,)), ((), ()))


def _params(*sem, vmem=VMEM_LIMIT):
    return pltpu.CompilerParams(dimension_semantics=sem, vmem_limit_bytes=vmem)


def _tile(n, preferred):
    t = min(n, preferred)
    while n % t:
        t //= 2
    return t


def _dot(a, b):
    return jnp.dot(a, b, preferred_element_type=F32)


def _dot_nt(a, b):
    return lax.dot_general(a, b, NT_DIMS, preferred_element_type=F32)


def _sigmoid(x):
    return 0.5 * jnp.tanh(0.5 * x) + 0.5


def _silu(x):
    return x * _sigmoid(x)


def _split_bf16(x, terms):
    out = []
    for _ in range(terms - 1):
        hi = x.astype(BF16)
        out.append(hi)
        x = x - hi.astype(F32)
    out.append(x.astype(BF16))
    return out


def _ada_kernel(c_ref, w_ref, b_ref, o_ref):
    a = _silu(c_ref[...])
    a_hi, a_lo = _split_bf16(a, 2)
    w_hi, w_lo = _split_bf16(w_ref[...], 2)
    acc = _dot(a_hi, w_hi) + _dot(a_lo, w_hi) + _dot(a_hi, w_lo)
    o_ref[...] = acc + b_ref[...]


def _ada(cond, w_ada, b_ada):
    depth, d, n = w_ada.shape
    tn = _tile(n, 512)
    rows = cond.shape[0]
    return pl.pallas_call(
        _ada_kernel,
        grid=(depth, n // tn),
        in_specs=[
            pl.BlockSpec((rows, d), lambda l, j: (0, 0)),
            pl.BlockSpec((None, d, tn), lambda l, j: (l, 0, j)),
            pl.BlockSpec((None, 1, tn), lambda l, j: (l, 0, j)),
        ],
        out_specs=pl.BlockSpec((None, rows, tn), lambda l, j: (l, 0, j)),
        out_shape=jax.ShapeDtypeStruct((depth, rows, n), F32),
        compiler_params=_params("parallel", "parallel"),
        name="ada_mod",
    )(cond, w_ada, b_ada.reshape(depth, 1, n))


def _in_proj_kernel(x_ref, shift_ref, scale_ref, gain_ref, w_ref, o_ref, h_ref):
    @pl.when(pl.program_id(1) == 0)
    def _():
        x = x_ref[...]
        ms = jnp.mean(x * x, axis=-1, keepdims=True)
        y = x * lax.rsqrt(ms + NORM_EPS) * gain_ref[...]
        h_ref[...] = (y * (1.0 + scale_ref[...]) + shift_ref[...]).astype(BF16)

    o_ref[...] = _dot(h_ref[...], w_ref[...])


def _in_proj(x2d, shift, scale, gain, w_in, layer, rows_per_cond):
    m, d = x2d.shape
    n = w_in.shape[-1]
    tm = _tile(rows_per_cond, 1024)
    tn = _tile(n, 1024)
    return pl.pallas_call(
        _in_proj_kernel,
        grid=(m // tm, n // tn),
        in_specs=[
            pl.BlockSpec((tm, d), lambda i, j: (i, 0)),
            pl.BlockSpec((None, 1, d), lambda i, j: (i * tm // rows_per_cond, 0, 0)),
            pl.BlockSpec((None, 1, d), lambda i, j: (i * tm // rows_per_cond, 0, 0)),
            pl.BlockSpec((1, d), lambda i, j: (0, 0)),
            pl.BlockSpec((None, d, tn), lambda i, j: (layer, 0, j)),
        ],
        out_specs=pl.BlockSpec((tm, tn), lambda i, j: (i, j)),
        out_shape=jax.ShapeDtypeStruct((m, n), F32),
        scratch_shapes=[pltpu.VMEM((tm, d), BF16)],
        compiler_params=_params("parallel", "arbitrary"),
        name="in_proj",
    )(x2d, shift, scale, gain, w_in)


def _head_norm(x, gain, group):
    x2 = x * x
    if group == LANES:
        ms = jnp.sum(x2, axis=-1, keepdims=True) * (1.0 / LANES)
    else:
        lo = lax.broadcasted_iota(jnp.int32, x.shape, 1) < group
        s_lo = jnp.sum(jnp.where(lo, x2, 0.0), axis=-1, keepdims=True)
        s_hi = jnp.sum(jnp.where(lo, 0.0, x2), axis=-1, keepdims=True)
        ms = jnp.where(lo, s_lo, s_hi) * (1.0 / group)
    return x * lax.rsqrt(ms + NORM_EPS) * gain


PREP_W = 2 * LANES


def _group_mean_matrix(group):
    i = np.arange(PREP_W)
    return ((i[:, None] // group) == (i[None, :] // group)).astype(np.float32) / group


def _rotate_half_matrix(d):
    quarter = d // 4
    i = np.arange(PREP_W)
    first = (i % (2 * quarter)) < quarter
    mat = np.zeros((PREP_W, PREP_W), np.float32)
    mat[i[first] + quarter, i[first]] = -1.0
    mat[i[~first] - quarter, i[~first]] = 1.0
    return mat


def _prep_kernel(*refs, rope, keep_f32, n_aliased):
    it = iter(refs)
    dq_ref, dk_ref, dv_ref, wq_ref, wk_ref, wv_ref = (next(it) for _ in range(6))
    gdq_ref, gdk_ref, gwq_ref, gwk_ref, md_ref, mw_ref = (next(it) for _ in range(6))
    if rope:
        rd_ref, rw_ref, cd_ref, sd_ref, cw_ref, sw_ref = (next(it) for _ in range(6))
    for _ in range(n_aliased):
        next(it)
    odq, odk, odv, owq, owk, owv = (next(it) for _ in range(6))
    if keep_f32:
        odk32, odv32, owk32, owv32 = (next(it) for _ in range(4))

    def pair(src, j, gain_ref, mean_ref, rot_ref, cos_ref, sin_ref):
        x = src[:, j * PREP_W:(j + 1) * PREP_W]
        hi, lo = _split_bf16(x * x, 2)
        ms = _dot(hi, mean_ref[...]) + _dot(lo, mean_ref[...])
        y = x * lax.rsqrt(ms + NORM_EPS) * gain_ref[...]
        if rope:
            y = y * cos_ref[...] + _dot(y.astype(BF16), rot_ref[...]) * sin_ref[...]
        return y

    dtabs = (rd_ref, cd_ref, sd_ref) if rope else (None, None, None)
    wtabs = (rw_ref, cw_ref, sw_ref) if rope else (None, None, None)
    for j in range(N_HEADS // 2):
        sl = slice(j * PREP_W, (j + 1) * PREP_W)
        q = pair(dq_ref, j, gdq_ref, md_ref, *dtabs)
        odq[:, sl] = (q * (DIFF_QK_DIM ** -0.5 * LOG2E)).astype(BF16)
        k = pair(dk_ref, j, gdk_ref, md_ref, *dtabs)
        odk[:, sl] = k.astype(BF16)
        if keep_f32:
            odk32[:, sl] = k
        q = pair(wq_ref, j, gwq_ref, mw_ref, *wtabs)
        owq[:, sl] = (q * (LANES ** -0.5 * LOG2E)).astype(BF16)
    for j in range(WIN_KV_HEADS // 2):
        sl = slice(j * PREP_W, (j + 1) * PREP_W)
        k = pair(wk_ref, j, gwk_ref, mw_ref, *wtabs)
        owk[:, sl] = k.astype(BF16)
        if keep_f32:
            owk32[:, sl] = k
    for j in range(N_HEADS):
        odv[j] = dv_ref[:, j * LANES:(j + 1) * LANES].T.astype(BF16)
    for j in range(WIN_KV_HEADS):
        for u in range(owv.shape[1]):
            owv[j, u] = wv_ref[u * WINDOW:(u + 1) * WINDOW, j * LANES:(j + 1) * LANES].T.astype(BF16)
    if keep_f32:
        odv32[...] = dv_ref[...]
        owv32[...] = wv_ref[...]


def _prep(proj, gains, rope_tabs, seq_len, caches=None, layer=0):
    m = proj.shape[0]
    keep_f32 = caches is not None
    tr = _tile(seq_len, DIFF_KV_CHUNK)
    nt = seq_len // tr
    w8, w4 = BRANCH_W, WIN_KV_HEADS * LANES
    rope = rope_tabs is not None

    def col(width, off):
        return pl.BlockSpec((tr, width), lambda i: (i, off // width))

    small = pl.BlockSpec((1, PREP_W), lambda i: (0, 0))
    square = pl.BlockSpec((PREP_W, PREP_W), lambda i: (0, 0))
    in_specs = [col(w8, OFF_DQ), col(w8, OFF_DK), col(w8, OFF_DV), col(w8, OFF_WQ),
                col(w4, OFF_WK), col(w4, OFF_WV), small, small, small, small, square, square]
    args = [proj] * 6 + list(gains) + [jnp.asarray(_group_mean_matrix(DIFF_QK_DIM), BF16),
                                        jnp.asarray(_group_mean_matrix(LANES), BF16)]
    if rope:
        tab = pl.BlockSpec((tr, PREP_W), lambda i: (i % nt, 0))
        in_specs += [square, square] + [tab] * 4
        args += [jnp.asarray(_rotate_half_matrix(DIFF_QK_DIM), BF16),
                 jnp.asarray(_rotate_half_matrix(LANES), BF16)] + list(rope_tabs)
    out8 = pl.BlockSpec((tr, w8), lambda i: (i, 0))
    out4 = pl.BlockSpec((tr, w4), lambda i: (i, 0))
    out_vt = pl.BlockSpec((None, N_HEADS, None, LANES, tr), lambda i: (i // nt, 0, i % nt, 0, 0))
    kb = tr // WINDOW
    out_wvt = pl.BlockSpec((None, WIN_KV_HEADS, kb, LANES, WINDOW), lambda i: (i // nt, 0, i % nt, 0, 0))
    out_specs = [out8, out8, out_vt, out8, out4, out_wvt]
    out_shape = [jax.ShapeDtypeStruct((m, w8), BF16), jax.ShapeDtypeStruct((m, w8), BF16),
                 jax.ShapeDtypeStruct((m // seq_len, N_HEADS, nt, LANES, tr), BF16),
                 jax.ShapeDtypeStruct((m, w8), BF16), jax.ShapeDtypeStruct((m, w4), BF16),
                 jax.ShapeDtypeStruct((m // seq_len, WIN_KV_HEADS, seq_len // WINDOW, LANES, WINDOW), BF16)]
    aliases = {}
    if keep_f32:
        for cache in caches:
            if not isinstance(cache, jax.ShapeDtypeStruct):
                aliases[len(args)] = len(out_shape)
                in_specs.append(pl.BlockSpec(memory_space=pl.ANY))
                args.append(cache)
            out_specs.append(pl.BlockSpec((None, None, tr, cache.shape[-1]),
                                          lambda i: (i // nt, layer, i % nt, 0)))
            out_shape.append(jax.ShapeDtypeStruct(cache.shape, F32))
    return pl.pallas_call(
        functools.partial(_prep_kernel, rope=rope, keep_f32=keep_f32, n_aliased=len(aliases)),
        grid=(m // tr,),
        in_specs=in_specs,
        out_specs=out_specs,
        out_shape=out_shape,
        input_output_aliases=aliases,
        compiler_params=_params("parallel"),
        name="qk_prep",
    )(*args)


def _diff_attn_kernel(*refs, lam_init, n_chunks, ck, has_ctx):
    it = iter(refs)
    lam_ref, subln_ref, q_ref, k_ref, vt_ref = (next(it) for _ in range(5))
    if has_ctx:
        kc_ref, vct_ref = next(it), next(it)
    g_ref, o_ref, acc_ref = next(it), next(it), next(it)

    tq = q_ref.shape[0]
    lv = lam_ref[...]
    lam = (jnp.exp(jnp.sum(lv[0:1] * lv[1:2], axis=-1, keepdims=True))
           - jnp.exp(jnp.sum(lv[2:3] * lv[3:4], axis=-1, keepdims=True)) + lam_init)
    first = lax.broadcasted_iota(jnp.int32, (tq, LANES), 1) < DIFF_QK_DIM

    for hh in range(vt_ref.shape[0]):
        sl = slice(hh * LANES, (hh + 1) * LANES)
        q = q_ref[:, sl]
        zero = jnp.zeros_like(q)
        q2 = jnp.concatenate([jnp.where(first, q, zero), jnp.where(first, zero, q)], axis=0)

        def step(kb, vtb, carry, is_first, hh=hh, q2=q2):
            m, l = carry
            s = _dot_nt(kb, q2)
            m_new = jnp.maximum(m, jnp.max(s, axis=0, keepdims=True))
            p = jnp.exp2(s - m_new)
            pv = _dot(vtb, p.astype(BF16))
            if is_first:
                l = jnp.sum(p, axis=0, keepdims=True)
                acc_ref[hh] = pv
            else:
                alpha = jnp.exp2(m - m_new)
                l = alpha * l + jnp.sum(p, axis=0, keepdims=True)
                acc_ref[hh] = alpha * acc_ref[hh] + pv
            return m_new, l

        none = jnp.zeros((1, 2 * tq), F32)
        carry = step(k_ref[pl.ds(0, ck), sl], vt_ref[hh, 0], (none, none), True)

        def body(c, carry, hh=hh, sl=sl, step=step):
            r = pl.multiple_of(c * ck, ck)
            return step(k_ref[pl.ds(r, ck), sl], vt_ref[hh, c], carry, False)

        carry = lax.fori_loop(1, n_chunks, body, carry)
        if has_ctx:
            carry = step(kc_ref[:, sl], vct_ref[hh], carry, False)

        r = acc_ref[hh] / carry[1]
        o = (r[:, :tq] - lam * r[:, tq:]).T
        ms = jnp.mean(o * o, axis=-1, keepdims=True)
        y = o * lax.rsqrt(ms + NORM_EPS) * subln_ref[...]
        y = y * (1.0 - lam_init)
        o_ref[:, sl] = (y * _silu(g_ref[:, sl])).astype(BF16)


def _diff_attn(q, k, vt, ctx, proj3, lam_vecs, subln, lam_init):
    b, t, _ = q.shape
    n_chunks, ck = vt.shape[2], vt.shape[4]
    tq = _tile(t, DIFF_Q_TILE)
    hps = max(1, min(N_HEADS, DIFF_Q_TILE // t))
    hw = hps * LANES
    has_ctx = ctx is not None
    qspec = pl.BlockSpec((None, tq, hw), lambda bi, h, i: (bi, i, h))
    in_specs = [pl.BlockSpec((4, DIFF_QK_DIM), lambda bi, h, i: (0, 0)),
                pl.BlockSpec((1, LANES), lambda bi, h, i: (0, 0)),
                qspec,
                pl.BlockSpec((None, t, hw), lambda bi, h, i: (bi, 0, h)),
                pl.BlockSpec((None, hps, n_chunks, LANES, ck), lambda bi, h, i: (bi, h, 0, 0, 0))]
    args = [lam_vecs, subln, q, k, vt]
    if has_ctx:
        p = ctx[0].shape[1]
        in_specs += [pl.BlockSpec((None, p, hw), lambda bi, h, i: (bi, 0, h)),
                     pl.BlockSpec((None, hps, LANES, p), lambda bi, h, i: (bi, h, 0, 0))]
        args += list(ctx)
    in_specs.append(pl.BlockSpec((None, tq, hw), lambda bi, h, i: (bi, i, OFF_DG // hw + h)))
    args.append(proj3)
    return pl.pallas_call(
        functools.partial(_diff_attn_kernel, lam_init=lam_init, n_chunks=n_chunks, ck=ck, has_ctx=has_ctx),
        grid=(b, N_HEADS // hps, t // tq),
        in_specs=in_specs,
        out_specs=qspec,
        out_shape=jax.ShapeDtypeStruct((b, t, BRANCH_W), BF16),
        scratch_shapes=[pltpu.VMEM((hps, LANES, 2 * tq), F32)],
        compiler_params=_params("parallel", "parallel", "parallel"),
        name="diff_attn",
    )(*args)


def _win_attn_kernel(*refs, banded, n_blocks, blocks_per_step, has_ctx):
    it = iter(refs)
    sink_ref, q_ref, k_ref, vt_ref = (next(it) for _ in range(4))
    if has_ctx:
        kc_ref, vct_ref = next(it), next(it)
    g_ref, o_ref = next(it), next(it)

    kvh = pl.program_id(1)
    w = WINDOW
    gw = WIN_GROUP * w
    lane = lax.broadcasted_iota(jnp.int32, (1, gw), 1)
    sink = jnp.where(lane < w, sink_ref[kvh * WIN_GROUP], sink_ref[kvh * WIN_GROUP + 1]) * LOG2E
    if banded:
        key = lax.broadcasted_iota(jnp.int32, (w, gw), 0)
        qry = lax.broadcasted_iota(jnp.int32, (w, gw), 1) & (w - 1)
    shared = []
    if not banded:
        shared += [(k_ref[j * w:(j + 1) * w, :], vt_ref[j], None) for j in range(n_blocks)]

    q2s = [jnp.concatenate([q_ref[u * w:(u + 1) * w, j * LANES:(j + 1) * LANES] for j in range(WIN_GROUP)],
                           axis=0) for u in range(blocks_per_step)]
    if has_ctx:
        s_ctx = _dot_nt(kc_ref[...], jnp.concatenate(q2s, axis=0))
    stats, p_ctx = [], []

    for u in range(blocks_per_step):
        cols = slice(u * gw, (u + 1) * gw)
        q2 = q2s[u]
        segs = []
        if banded:
            n = pl.program_id(2) * blocks_per_step + u
            b_prev = jnp.maximum(n - 1, 0)
            b_next = jnp.minimum(n + 1, n_blocks - 1)
            keep_prev = (key >= qry) & (n > 0)
            keep_next = (key <= qry) & (n < n_blocks - 1)
            segs = [(k_ref[pl.ds(pl.multiple_of(b_prev * w, w), w), :], vt_ref[b_prev], keep_prev),
                    (k_ref[pl.ds(pl.multiple_of(n * w, w), w), :], vt_ref[n], None),
                    (k_ref[pl.ds(pl.multiple_of(b_next * w, w), w), :], vt_ref[b_next], keep_next)]
        segs += shared
        scores = []
        for kb, _, keep in segs:
            s = _dot_nt(kb, q2)
            scores.append(s if keep is None else jnp.where(keep, s, -jnp.inf))
        m = sink
        for s in scores:
            m = jnp.maximum(m, jnp.max(s, axis=0, keepdims=True))
        if has_ctx:
            m = jnp.maximum(m, jnp.max(s_ctx[:, cols], axis=0, keepdims=True))
        l = jnp.exp2(sink - m)
        acc = None
        for s, (_, vtb, _) in zip(scores, segs):
            p = jnp.exp2(s - m)
            l = l + jnp.sum(p, axis=0, keepdims=True)
            pv = _dot(vtb, p.astype(BF16))
            acc = pv if acc is None else acc + pv
        if has_ctx:
            p = jnp.exp2(s_ctx[:, cols] - m)
            l = l + jnp.sum(p, axis=0, keepdims=True)
            p_ctx.append(p.astype(BF16))
        stats.append((acc, l))

    if has_ctx:
        pv_ctx = _dot(vct_ref[...], jnp.concatenate(p_ctx, axis=1))
    for u in range(blocks_per_step):
        rows = slice(u * w, (u + 1) * w)
        acc, l = stats[u]
        if has_ctx:
            acc = acc + pv_ctx[:, u * gw:(u + 1) * gw]
        o = (acc / l).T
        for j in range(WIN_GROUP):
            sl = slice(j * LANES, (j + 1) * LANES)
            o_ref[rows, sl] = (o[j * w:(j + 1) * w] * _silu(g_ref[rows, sl])).astype(BF16)


def _win_attn(q, k, vt, ctx, proj3, sink, banded):
    b, t, _ = q.shape
    nb = t // WINDOW
    gw = WIN_GROUP * LANES
    has_ctx = ctx is not None
    per_step = _tile(nb, WIN_BLOCKS_PER_STEP)
    tq = per_step * WINDOW
    qspec = pl.BlockSpec((None, tq, gw), lambda bi, h, i: (bi, i, h))
    in_specs = [pl.BlockSpec(memory_space=pltpu.SMEM), qspec,
                pl.BlockSpec((None, t, LANES), lambda bi, h, i: (bi, 0, h)),
                pl.BlockSpec((None, None, nb, LANES, WINDOW), lambda bi, h, i: (bi, h, 0, 0, 0))]
    args = [sink, q, k, vt]
    if has_ctx:
        p = ctx[0].shape[1]
        in_specs += [pl.BlockSpec((None, p, LANES), lambda bi, h, i: (bi, 0, h)),
                     pl.BlockSpec((None, None, LANES, p), lambda bi, h, i: (bi, h, 0, 0))]
        args += list(ctx)
    in_specs.append(pl.BlockSpec((None, tq, gw), lambda bi, h, i: (bi, i, OFF_WG // gw + h)))
    args.append(proj3)
    return pl.pallas_call(
        functools.partial(_win_attn_kernel, banded=banded, n_blocks=nb, blocks_per_step=per_step,
                          has_ctx=has_ctx),
        grid=(b, WIN_KV_HEADS, nb // per_step),
        in_specs=in_specs,
        out_specs=qspec,
        out_shape=jax.ShapeDtypeStruct((b, t, BRANCH_W), BF16),
        compiler_params=_params("parallel", "parallel", "parallel"),
        name="win_attn",
    )(*args)


def _hgrn_tables(reverse):
    c = HG_CHUNK
    t = np.arange(c)[:, None]
    u = np.arange(c)[None, :]
    mats, masks = [], []
    for m in range(HG_LEVELS):
        half, blk = 1 << m, 2 << m
        r = t - t % blk + half - 1
        right = (t % blk) >= half
        if not reverse:
            e_q = (u >= r + 1) & (u <= t)
            e_k = (u >= t + 1) & (u <= r)
            mats.append(np.where(right, e_q, e_k))
            masks.append(right & ~right.T & (t // blk == u // blk))
        else:
            e_q = (u >= t) & (u <= r)
            e_k = (u >= r + 1) & (u <= t - 1)
            mats.append(np.where(right, e_k, e_q))
            masks.append(~right & right.T & (t // blk == u // blk))
    mats = mats[:HG_MXU_LEVELS] + [(u <= t) if not reverse else (u >= t)]
    masks.append(t == u)
    return (np.concatenate(mats, 0).astype(np.float32), np.stack(masks).astype(np.float32))


def _hgrn_constants():
    c = HG_CHUNK
    tabs = [_hgrn_tables(False), _hgrn_tables(True)]
    mats = np.stack([np.tile(tb[0], (1, 3)) for tb in tabs])
    masks = np.zeros((HG_LEVELS + 1, 4, c, LANES), np.float32)
    for a in range(4):
        lo = (a % 2) * c
        masks[:, a, :, lo:lo + c] = tabs[a // 2][1]
    return mats, masks


def _hgrn_kernel(lb_ref, s0_ref, mat_ref, mask_ref,
                 qf_ref, vf_ref, zf_ref, qb_ref, vb_ref, zb_ref, *rest, layer, n_chunks):
    of_ref, ob_ref, sout_ref, st_ref = rest[-4:]
    i = pl.program_id(2)
    c = HG_CHUNK
    pair = 2
    pw = pair * LANES

    @pl.when(i == 0)
    def _():
        for d in range(2):
            for hh in range(pair):
                st_ref[2 * d + hh] = s0_ref[d, hh]

    def lower_bound(d):
        rows = [lb_ref[d, r:r + 1, :] for r in range(lb_ref.shape[1])]
        mx = functools.reduce(jnp.maximum, rows)
        es = [jnp.exp(r - mx) for r in rows]
        den = functools.reduce(lambda a, b_: a + b_, es)
        ps = [e / den for e in es]
        return functools.reduce(lambda a, b_: a + b_, ps[:layer + 1]) - ps[0]

    bit_row = lax.broadcasted_iota(jnp.int32, (c, pw), 0)

    def side_rows(m, d, on_query_rows, on_key_rows):
        half = 1 << m
        if half % 8:
            return jnp.where(((bit_row >> m) & 1) == (1 - d), on_query_rows, on_key_rows)
        slabs = [(on_query_rows if (r0 // half) % 2 == 1 - d else on_key_rows)[r0:r0 + half]
                 for r0 in range(0, c, half)]
        return jnp.concatenate(slabs, axis=0)

    def decays(d, z, lb):
        e = jnp.exp(-jnp.abs(z))
        r = 1.0 / (1.0 + e)
        er = e * r
        pos = z >= 0.0
        f = lb + (1.0 - lb) * jnp.where(pos, r, er)
        kk = (1.0 - lb) * jnp.where(pos, er, r)
        logf = jnp.log(jnp.maximum(f, F_FLOOR)) * LOG2E
        parts = jnp.concatenate(_split_bf16(logf, 3), axis=0)
        e_small = _dot(mat_ref[d], parts)
        run = e_small[HG_MXU_LEVELS * c:]
        levels = [jnp.exp2(e_small[m * c:(m + 1) * c]) for m in range(HG_MXU_LEVELS)]
        for m in range(HG_MXU_LEVELS, HG_LEVELS):
            half, blk = 1 << m, 2 << m
            edge = half - 1 + d
            rho = jnp.concatenate(
                [jnp.broadcast_to(run[b0 + edge:b0 + edge + 1], (blk, pw)) for b0 in range(0, c, blk)], axis=0)
            diff = run - rho
            levels.append(jnp.exp2(side_rows(m, d, diff, -diff)))
        last = (c - 1) if d == 0 else 0
        total = run[last:last + 1]
        return kk, levels, jnp.exp2(run), jnp.exp2(total - run), jnp.exp2(total)

    lbs = (lower_bound(0), lower_bound(1))

    def body(ci, _):
        rows = (pl.ds(pl.multiple_of(ci * c, c), c), pl.ds(pl.multiple_of((n_chunks - 1 - ci) * c, c), c))
        qs = (qf_ref[rows[0], :], qb_ref[rows[1], :])
        vs = (vf_ref[rows[0], :].astype(BF16), vb_ref[rows[1], :].astype(BF16))
        zs = (zf_ref[rows[0], :], zb_ref[rows[1], :])
        dec = [decays(d, zs[d], lbs[d]) for d in range(2)]

        def stacked(fn):
            wide = [fn(d).astype(BF16) for d in range(2)]
            return jnp.concatenate([wide[a // 2][:, (a % 2) * LANES:(a % 2 + 1) * LANES] for a in range(4)],
                                   axis=0)

        def pair_weights(g, m):
            return [mask_ref[m, a] * g[a * c:(a + 1) * c, (a // 2) * LANES:(a // 2 + 1) * LANES]
                    for a in range(4)]

        acc = pair_weights(_dot_nt(stacked(lambda d: qs[d]), stacked(lambda d: dec[d][0])), HG_LEVELS)
        for m in range(HG_LEVELS):
            def level_rows(d, m=m):
                return side_rows(m, d, qs[d], dec[d][0]) * dec[d][1][m]
            lm = stacked(level_rows)
            acc = [x + y for x, y in zip(acc, pair_weights(_dot_nt(lm, lm), m))]

        outs = (of_ref, ob_ref)
        for a in range(4):
            d, hh = a // 2, a % 2
            sl = slice(hh * LANES, (hh + 1) * LANES)
            kk, _, from_start, to_end, whole = dec[d]
            v = vs[d][:, sl]
            qe = (qs[d][:, sl] * from_start[:, sl]).astype(BF16)
            ke = (kk[:, sl] * to_end[:, sl]).astype(BF16)
            st = st_ref[a]
            lhs = jnp.concatenate([acc[a].astype(BF16), qe], axis=1)
            rhs = jnp.concatenate([v, v, st.astype(BF16)], axis=0)
            outs[d][rows[d], sl] = _dot(lhs, rhs)
            decay_col = jnp.broadcast_to(whole[:, sl], (LANES, LANES)).T
            st_ref[a] = st * decay_col + lax.dot_general(ke, v, TN_DIMS, preferred_element_type=F32)
        return 0

    lax.fori_loop(0, n_chunks, body, 0, unroll=8)

    @pl.when(i == pl.num_programs(2) - 1)
    def _():
        for d in range(2):
            for hh in range(pair):
                sout_ref[d, hh] = st_ref[2 * d + hh]


def _hgrn(proj3, hg_lb, state0, layer, states_out=None):
    b, t, _ = proj3.shape
    tb = _tile(t, 1024)
    nt = t // tb
    pw = 2 * LANES
    mats, masks = _hgrn_constants()
    mats = jnp.asarray(mats, BF16)
    masks = jnp.asarray(masks, F32)

    def fwd(off):
        return pl.BlockSpec((None, tb, pw), lambda bi, hp, i: (bi, i, off // pw + hp))

    def bwd(off):
        return pl.BlockSpec((None, tb, pw), lambda bi, hp, i: (bi, nt - 1 - i, off // pw + hp))

    st_spec = pl.BlockSpec((None, 2, 2, LANES, LANES), lambda bi, hp, i: (bi, 0, hp, 0, 0))
    in_specs = [
        pl.BlockSpec((2, hg_lb.shape[1], pw), lambda bi, hp, i: (0, 0, hp)),
        st_spec,
        pl.BlockSpec(mats.shape, lambda bi, hp, i: (0, 0, 0)),
        pl.BlockSpec(masks.shape, lambda bi, hp, i: (0, 0, 0, 0)),
        fwd(OFF_HQ), fwd(OFF_HI), fwd(OFF_HFF),
        bwd(OFF_HQ), bwd(OFF_HI), bwd(OFF_HFB),
    ]
    args = [hg_lb, state0, mats, masks, proj3, proj3, proj3, proj3, proj3, proj3]
    aliases = {}
    if states_out is None:
        st_out_spec, st_out_shape = st_spec, jax.ShapeDtypeStruct(state0.shape, F32)
    else:
        if not isinstance(states_out, jax.ShapeDtypeStruct):
            aliases[len(args)] = 2
            in_specs.append(pl.BlockSpec(memory_space=pl.ANY))
            args.append(states_out)
        st_out_spec = pl.BlockSpec((None, None, 2, 2, LANES, LANES), lambda bi, hp, i: (bi, layer, 0, hp, 0, 0))
        st_out_shape = jax.ShapeDtypeStruct(states_out.shape, F32)
    return pl.pallas_call(
        functools.partial(_hgrn_kernel, layer=layer, n_chunks=tb // HG_CHUNK),
        grid=(b, N_HEADS // 2, nt),
        in_specs=in_specs,
        out_specs=[
            pl.BlockSpec((None, tb, pw), lambda bi, hp, i: (bi, i, hp)),
            pl.BlockSpec((None, tb, pw), lambda bi, hp, i: (bi, nt - 1 - i, hp)),
            st_out_spec,
        ],
        out_shape=[
            jax.ShapeDtypeStruct((b, t, BRANCH_W), F32),
            jax.ShapeDtypeStruct((b, t, BRANCH_W), F32),
            st_out_shape,
        ],
        scratch_shapes=[pltpu.VMEM((4, LANES, LANES), F32)],
        input_output_aliases=aliases,
        compiler_params=_params("parallel", "parallel", "arbitrary"),
        name="hgrn_scan",
    )(*args)


def _merge_kernel(yd_ref, of_ref, ob_ref, yw_ref, hg_ref, gain_ref, mg0_ref, mg1_ref, mg2_ref, w_ref,
                  o_ref, yh_ref):
    @pl.when(pl.program_id(1) == 0)
    def _():
        for j in range(N_HEADS):
            sl = slice(j * LANES, (j + 1) * LANES)
            y = _head_norm(of_ref[:, sl] + ob_ref[:, sl], gain_ref[...], LANES)
            yh_ref[:, sl] = (y * _silu(hg_ref[:, sl])).astype(BF16)

    acc = _sigmoid(mg0_ref[...]) * _dot(yd_ref[...], w_ref[0])
    acc = acc + _sigmoid(mg1_ref[...]) * _dot(yh_ref[...], w_ref[1])
    acc = acc + _sigmoid(mg2_ref[...]) * _dot(yw_ref[...], w_ref[2])
    o_ref[...] = acc.astype(BF16)


def _merge(yd, o_f, o_b, yw, proj, hg_gain, w_branch, layer):
    m = yd.shape[0]
    d = w_branch.shape[-1]
    tm = _tile(m, 1024)
    tn = _tile(d, 512)
    row = lambda width: pl.BlockSpec((tm, width), lambda i, j: (i, 0))

    def mg(nb):
        return pl.BlockSpec((tm, tn), lambda i, j: (i, (OFF_MG + nb * d) // tn + j))

    return pl.pallas_call(
        _merge_kernel,
        grid=(m // tm, d // tn),
        in_specs=[row(BRANCH_W), row(BRANCH_W), row(BRANCH_W), row(BRANCH_W),
                  pl.BlockSpec((tm, BRANCH_W), lambda i, j: (i, OFF_HG // BRANCH_W)),
                  pl.BlockSpec((1, LANES), lambda i, j: (0, 0)),
                  mg(0), mg(1), mg(2),
                  pl.BlockSpec((None, 3, BRANCH_W, tn), lambda i, j: (layer, 0, 0, j))],
        out_specs=pl.BlockSpec((tm, tn), lambda i, j: (i, j)),
        out_shape=jax.ShapeDtypeStruct((m, d), BF16),
        scratch_shapes=[pltpu.VMEM((tm, BRANCH_W), BF16)],
        compiler_params=_params("parallel", "arbitrary", vmem=VMEM_LIMIT_WIDE),
        name="branch_merge",
    )(yd, o_f, o_b, yw, proj, hg_gain, proj, proj, proj, w_branch)


def _out_proj_kernel(x_ref, gate_ref, m_ref, w_ref, o_ref):
    o_ref[...] = x_ref[...] + gate_ref[...] * _dot(m_ref[...], w_ref[...])


def _out_proj(x2d, gate, merged, w_out, layer, rows_per_cond):
    m, d = x2d.shape
    tm = _tile(rows_per_cond, 2048)
    tn = _tile(d, 512)
    return pl.pallas_call(
        _out_proj_kernel,
        grid=(m // tm, d // tn),
        in_specs=[
            pl.BlockSpec((tm, tn), lambda i, j: (i, j)),
            pl.BlockSpec((None, 1, tn), lambda i, j: (i * tm // rows_per_cond, 0, j)),
            pl.BlockSpec((tm, d), lambda i, j: (i, 0)),
            pl.BlockSpec((None, d, tn), lambda i, j: (layer, 0, j)),
        ],
        out_specs=pl.BlockSpec((tm, tn), lambda i, j: (i, j)),
        out_shape=jax.ShapeDtypeStruct((m, d), F32),
        compiler_params=_params("parallel", "parallel"),
        name="out_proj",
    )(x2d, gate, merged, w_out)


def _rope_tables(t, d):
    rows = t // GRID_W
    row = jnp.repeat(jnp.arange(rows, dtype=F32), GRID_W)
    col = jnp.tile(jnp.arange(GRID_W, dtype=F32), rows)
    quarter = d // 4
    inv = ROPE_BASE ** (-jnp.arange(quarter, dtype=F32) / quarter)
    ar = row[:, None] * inv
    ac = col[:, None] * inv
    ang = jnp.concatenate([ar, ar, ac, ac], axis=-1)
    return tuple(jnp.tile(a, (1, PREP_W // d)) for a in (jnp.cos(ang), jnp.sin(ang)))


def _tile_gain(g, width=LANES):
    return jnp.tile(g, width // g.shape[0]).reshape(1, width)


def _layer(x, mod, l, w, rope_tabs, caches, state0, new_caches):
    is_context = new_caches is not None
    b, t, d = x.shape
    m = b * t
    rows_per_cond = m // mod.shape[0]
    x2d = x.reshape(m, d)
    shift, scale, gate = (mod[:, None, j * d:(j + 1) * d] for j in range(3))
    proj = _in_proj(x2d, shift, scale, w["norm_gain"][l].reshape(1, d), w["w_in"], l, rows_per_cond)
    gains = tuple(_tile_gain(w[name][l], PREP_W)
                  for name in ("diff_q_norm", "diff_k_norm", "win_q_norm", "win_k_norm"))
    prepped = _prep(proj, gains, rope_tabs, t, new_caches[:4] if is_context else None, l)
    dq, dk, wq, wk = (prepped[j].reshape(b, t, -1) for j in (0, 1, 3, 4))
    dv, wv = prepped[2], prepped[5]
    proj3 = proj.reshape(b, t, -1)
    lam_init = 0.8 - 0.6 * math.exp(-0.3 * l)
    if is_context:
        dctx = wctx = None
    else:
        ck_d, cv_d, ck_w, cv_w = caches
        dctx = (ck_d.reshape(b, -1, BRANCH_W).astype(BF16),
                jnp.transpose(cv_d, (0, 2, 3, 1)).astype(BF16))
        wctx = (ck_w.reshape(b, -1, WIN_KV_HEADS * LANES).astype(BF16),
                jnp.transpose(cv_w, (0, 2, 3, 1)).astype(BF16))
    yd = _diff_attn(dq, dk, dv, dctx, proj3, w["diff_lambda"][l], _tile_gain(w["diff_subln"][l]), lam_init)
    o_f, o_b, s_out = _hgrn(proj3, w["hg_lb"], state0, l, new_caches[4] if is_context else None)
    yw = _win_attn(wq, wk, wv, wctx, proj3, w["win_sink"][l], banded=not is_context)
    merged = _merge(yd.reshape(m, -1), o_f.reshape(m, -1), o_b.reshape(m, -1), yw.reshape(m, -1), proj,
                    _tile_gain(w["hg_out_norm"][l]), w["w_branch"], l)
    y = _out_proj(x2d, gate, merged, w["w_out"], l, rows_per_cond).reshape(b, t, d)
    if not is_context:
        return y, None
    return y, tuple(prepped[6:10]) + (s_out,)


def kernel(x_prompt, x_sample, cache_diff_k, cache_diff_v, cache_win_k, cache_win_v, state_hgrn, c, c_ctx, norm_gain, w_ada, b_ada, w_in, diff_q_norm, diff_k_norm, diff_lambda, diff_subln, hg_lb, hg_out_norm, win_q_norm, win_k_norm, win_sink, w_branch, w_out):
    depth = w_in.shape[0]
    dec_b, dec_t, d = x_sample.shape
    w = {"norm_gain": norm_gain, "w_in": w_in.astype(BF16), "diff_q_norm": diff_q_norm,
         "diff_k_norm": diff_k_norm, "diff_lambda": diff_lambda, "diff_subln": diff_subln,
         "hg_lb": hg_lb, "hg_out_norm": hg_out_norm, "win_q_norm": win_q_norm, "win_k_norm": win_k_norm,
         "win_sink": win_sink, "w_branch": w_branch.astype(BF16), "w_out": w_out.astype(BF16)}

    cond_rows = 8 * ((1 + dec_b + 7) // 8)
    cond = jnp.zeros((cond_rows, d), F32).at[0].set(c_ctx).at[1:1 + dec_b].set(c)
    mod = _ada(cond, w_ada, b_ada)

    y_prompt = x_prompt
    pb, pt = x_prompt.shape[:2]
    zero_state = jnp.zeros((pb,) + state_hgrn.shape[2:], F32)
    new = tuple(jax.ShapeDtypeStruct((pb, depth, pt, width), F32)
                for width in (BRANCH_W, BRANCH_W, WIN_KV_HEADS * LANES, WIN_KV_HEADS * LANES))
    new += (jax.ShapeDtypeStruct((pb, depth) + state_hgrn.shape[2:], F32),)
    for l in range(depth):
        y_prompt, new = _layer(y_prompt, mod[l, 0:1], l, w, None, None, zero_state, new)

    ropes = _rope_tables(dec_t, DIFF_QK_DIM) + _rope_tables(dec_t, LANES)
    y_sample = x_sample
    for l in range(depth):
        caches = (cache_diff_k[:, l], cache_diff_v[:, l], cache_win_k[:, l], cache_win_v[:, l])
        y_sample, _ = _layer(y_sample, mod[l, 1:1 + dec_b], l, w, ropes, caches, state_hgrn[:, l], None)

    return (y_prompt, y_sample,
            new[0].reshape(pb, depth, pt, N_HEADS, 2, DIFF_QK_DIM),
            new[1].reshape(pb, depth, pt, N_HEADS, LANES),
            new[2].reshape(pb, depth, pt, WIN_KV_HEADS, LANES),
            new[3].reshape(pb, depth, pt, WIN_KV_HEADS, LANES),
            new[4])
```
